```python
import jax, jax.numpy as jnp
from jax import lax
import numpy as np


D_MODEL = 1024
BATCH = 8
SEQ = 4096
DEPTH = 1

POOL_WIDTH = D_MODEL // 2
POOL_WINDOWS = (2, 4, 8, 16)
POOL_GROUPS = len(POOL_WINDOWS)
POOL_GROUP_DIM = POOL_WIDTH // POOL_GROUPS
HEAD_DIM = 64
ATTN_WIDTH = D_MODEL - POOL_WIDTH
ATTN_HEADS = ATTN_WIDTH // HEAD_DIM
MIX_WIDTH = POOL_WIDTH + ATTN_WIDTH
MOBA_BLOCK = 256
MOBA_TOPK = 3
Q_CHUNK = 128
ROPE_THETA = 10000.0
MEM_LEN = 256
XATTN_HEADS = 4
XATTN_HEAD_DIM = D_MODEL // XATTN_HEADS
D_FF = 2816
CONV_WIDTH = 3
EPS = 1e-6

kernel_name = 'hybrid_pool_moba_xattn_convffn'


def _rms_norm(x, g):
    xf = x.astype(jnp.float32)
    y = xf * lax.rsqrt(jnp.mean(xf * xf, axis=-1, keepdims=True) + EPS)
    return (y * g.astype(jnp.float32)).astype(x.dtype)


def _rope(x, pos):
    half = HEAD_DIM // 2
    inv_freq = ROPE_THETA ** (-jnp.arange(half, dtype=jnp.float32) / half)
    ang = pos.astype(jnp.float32)[:, None] * inv_freq[None, :]
    cos = jnp.cos(ang).astype(x.dtype)
    sin = jnp.sin(ang).astype(x.dtype)
    x1, x2 = x[..., :half], x[..., half:]
    return jnp.concatenate([x1 * cos - x2 * sin, x2 * cos + x1 * sin], axis=-1)


def _pool_mixer(u, pool_w, pool_scale):
    B, S, _ = u.shape
    uf = u.astype(jnp.float32)
    cs = jnp.cumsum(uf, axis=1)
    t = jnp.arange(S)
    outs = []
    for g, w in enumerate(POOL_WINDOWS):
        sl = slice(g * POOL_GROUP_DIM, (g + 1) * POOL_GROUP_DIM)
        c = cs[..., sl]
        c_prev = jnp.pad(c, ((0, 0), (w, 0), (0, 0)))[:, :S]
        cnt = jnp.minimum(t + 1, w).astype(jnp.float32)
        outs.append((c - c_prev) / cnt[None, :, None] - uf[..., sl])
    pooled = jnp.stack(outs, axis=2).astype(u.dtype)
    mixed = jnp.einsum('bsgc,gcd->bsgd', pooled, pool_w)
    return mixed.reshape(B, S, POOL_WIDTH) * pool_scale


def _moba_attention(q, k, v):
    B, H, S, Dh = q.shape
    n_blocks = max(-(-S // MOBA_BLOCK), MOBA_TOPK)
    pad = n_blocks * MOBA_BLOCK - S
    kb = jnp.pad(k, ((0, 0), (0, 0), (0, pad), (0, 0))).reshape(B, H, n_blocks, MOBA_BLOCK, Dh)
    vb = jnp.pad(v, ((0, 0), (0, 0), (0, pad), (0, 0))).reshape(B, H, n_blocks, MOBA_BLOCK, Dh)
    k_mean = jnp.mean(kb.astype(jnp.float32), axis=3).astype(k.dtype)
    n_chunks = S // Q_CHUNK
    qc = q.reshape(B, H, n_chunks, Q_CHUNK, Dh).transpose(0, 2, 1, 3, 4)
    scale = HEAD_DIM ** -0.5
    h_idx = jnp.arange(H)[:, None, None]
    blk_ids = jnp.arange(n_blocks)
    slot_ids = jnp.arange(MOBA_TOPK)
    n_sel = MOBA_TOPK * MOBA_BLOCK

    def chunk_fn(q_c, c, kb_b, vb_b, km_b):
        q0 = c * Q_CHUNK
        qblk = q0 // MOBA_BLOCK
        q_pos = q0 + jnp.arange(Q_CHUNK)
        gate = jnp.einsum('hqd,hnd->hqn', q_c, km_b, preferred_element_type=jnp.float32)
        gate = jnp.where(blk_ids[None, None, :] < qblk, gate, -jnp.inf)
        _, sel = lax.top_k(gate, MOBA_TOPK)
        k_sel = kb_b[h_idx, sel]
        v_sel = vb_b[h_idx, sel]
        s_sel = jnp.einsum('hqd,hqkjd->hqkj', q_c, k_sel, preferred_element_type=jnp.float32) * scale
        s_sel = jnp.where((slot_ids < qblk)[None, None, :, None], s_sel, -jnp.inf)
        k_own = lax.dynamic_index_in_dim(kb_b, qblk, axis=1, keepdims=False)
        v_own = lax.dynamic_index_in_dim(vb_b, qblk, axis=1, keepdims=False)
        s_own = jnp.einsum('hqd,hjd->hqj', q_c, k_own, preferred_element_type=jnp.float32) * scale
        k_pos = qblk * MOBA_BLOCK + jnp.arange(MOBA_BLOCK)
        s_own = jnp.where((k_pos[None, :] <= q_pos[:, None])[None], s_own, -jnp.inf)
        scores = jnp.concatenate([s_sel.reshape(H, Q_CHUNK, n_sel), s_own], axis=-1)
        p = jax.nn.softmax(scores, axis=-1).astype(v_sel.dtype)
        p_sel = p[..., :n_sel].reshape(H, Q_CHUNK, MOBA_TOPK, MOBA_BLOCK)
        p_own = p[..., n_sel:]
        return (jnp.einsum('hqkj,hqkjd->hqd', p_sel, v_sel)
                + jnp.einsum('hqj,hjd->hqd', p_own, v_own))

    def batch_fn(args):
        q_b, kb_b, vb_b, km_b = args
        return lax.map(lambda a: chunk_fn(a[0], a[1], kb_b, vb_b, km_b),
                       (q_b, jnp.arange(n_chunks)))

    out = lax.map(batch_fn, (qc, kb, vb, k_mean))
    return out.transpose(0, 2, 1, 3, 4).reshape(B, H, S, Dh)


def _cross_attention(h, m, w_xq, w_xkv, xq_norm_g, xk_norm_g, w_xo):
    B, S, _ = h.shape
    M = m.shape[1]
    q = (h @ w_xq).reshape(B, S, XATTN_HEADS, XATTN_HEAD_DIM)
    kv = (m @ w_xkv).reshape(B, M, 2, XATTN_HEADS, XATTN_HEAD_DIM)
    k, v = kv[:, :, 0], kv[:, :, 1]
    q = _rms_norm(q, xq_norm_g)
    k = _rms_norm(k, xk_norm_g)
    s = jnp.einsum('bshd,bmhd->bhsm', q, k, preferred_element_type=jnp.float32) * (XATTN_HEAD_DIM ** -0.5)
    p = jax.nn.softmax(s, axis=-1).astype(v.dtype)
    o = jnp.einsum('bhsm,bmhd->bshd', p, v).reshape(B, S, D_MODEL)
    return o @ w_xo


def _conv_ffn(h, w_up, conv_w, conv_b, w_down):
    up = h @ w_up
    C = up.shape[-1]
    rhs = conv_w.reshape(CONV_WIDTH, 1, C).astype(up.dtype)
    up = lax.conv_general_dilated(up, rhs, window_strides=(1,), padding=[(CONV_WIDTH - 1, 0)],
                                  dimension_numbers=('NWC', 'WIO', 'NWC'),
                                  feature_group_count=C) + conv_b
    gate, val = up[..., :D_FF], up[..., D_FF:]
    return (jax.nn.silu(gate) * val) @ w_down


def setup_inputs(seed: int = 0) -> dict:
    key = jax.random.key(seed)
    ks = jax.random.split(key, 24)
    f32 = jnp.float32

    def nrm(k, shape, scale):
        return jax.random.normal(k, shape, f32) * scale

    def gain(k, shape):
        return 1.0 + 0.1 * jax.random.normal(k, shape, f32)

    L = DEPTH
    return {
        'x': jax.random.normal(ks[0], (BATCH, SEQ, D_MODEL), f32),
        'mem': jax.random.normal(ks[1], (BATCH, MEM_LEN, D_MODEL), f32),
        'norm_mix_g': gain(ks[2], (L, D_MODEL)),
        'w_in': nrm(ks[3], (L, D_MODEL, POOL_WIDTH + 3 * ATTN_WIDTH), D_MODEL ** -0.5),
        'pool_w': nrm(ks[4], (L, POOL_GROUPS, POOL_GROUP_DIM, POOL_GROUP_DIM), POOL_GROUP_DIM ** -0.5),
        'pool_scale': gain(ks[5], (L, POOL_WIDTH)),
        'q_norm_g': gain(ks[6], (L, HEAD_DIM)),
        'k_norm_g': gain(ks[7], (L, HEAD_DIM)),
        'w_out': nrm(ks[8], (L, MIX_WIDTH, D_MODEL), MIX_WIDTH ** -0.5),
        'norm_xattn_g': gain(ks[9], (L, D_MODEL)),
        'norm_mem_g': gain(ks[10], (L, D_MODEL)),
        'w_xq': nrm(ks[11], (L, D_MODEL, D_MODEL), D_MODEL ** -0.5),
        'w_xkv': nrm(ks[12], (L, D_MODEL, 2 * D_MODEL), D_MODEL ** -0.5),
        'xq_norm_g': gain(ks[13], (L, XATTN_HEAD_DIM)),
        'xk_norm_g': gain(ks[14], (L, XATTN_HEAD_DIM)),
        'w_xo': nrm(ks[15], (L, D_MODEL, D_MODEL), D_MODEL ** -0.5),
        'norm_ffn_g': gain(ks[16], (L, D_MODEL)),
        'w_up': nrm(ks[17], (L, D_MODEL, 2 * D_FF), D_MODEL ** -0.5),
        'conv_w': nrm(ks[18], (L, CONV_WIDTH, 2 * D_FF), CONV_WIDTH ** -0.5),
        'conv_b': nrm(ks[19], (L, 2 * D_FF), 0.01),
        'w_down': nrm(ks[20], (L, D_FF, D_MODEL), D_FF ** -0.5),
    }


def reference(x, mem, norm_mix_g, w_in, pool_w, pool_scale, q_norm_g, k_norm_g, w_out,
              norm_xattn_g, norm_mem_g, w_xq, w_xkv, xq_norm_g, xk_norm_g, w_xo,
              norm_ffn_g, w_up, conv_w, conv_b, w_down):
    B, S, _ = x.shape
    pos = jnp.arange(S)
    for l in range(DEPTH):
        h = _rms_norm(x, norm_mix_g[l])
        proj = h @ w_in[l]
        u = proj[..., :POOL_WIDTH]
        qkv = proj[..., POOL_WIDTH:].reshape(B, S, 3, ATTN_HEADS, HEAD_DIM)
        q = qkv[:, :, 0].transpose(0, 2, 1, 3)
        k = qkv[:, :, 1].transpose(0, 2, 1, 3)
        v = qkv[:, :, 2].transpose(0, 2, 1, 3)
        q = _rope(_rms_norm(q, q_norm_g[l]), pos)
        k = _rope(_rms_norm(k, k_norm_g[l]), pos)
        attn = _moba_attention(q, k, v).transpose(0, 2, 1, 3).reshape(B, S, ATTN_WIDTH)
        pool = _pool_mixer(u, pool_w[l], pool_scale[l])
        x = x + jnp.concatenate([pool, attn], axis=-1) @ w_out[l]
        h = _rms_norm(x, norm_xattn_g[l])
        m = _rms_norm(mem, norm_mem_g[l])
        x = x + _cross_attention(h, m, w_xq[l], w_xkv[l], xq_norm_g[l], xk_norm_g[l], w_xo[l])
        h = _rms_norm(x, norm_ffn_g[l])
        x = x + _conv_ffn(h, w_up[l], conv_w[l], conv_b[l], w_down[l])
    return x
```

```python
import functools

import jax
import jax.numpy as jnp
from jax import lax
from jax.experimental import pallas as pl
from jax.experimental.pallas import tpu as pltpu

F32 = jnp.float32
BF16 = jnp.bfloat16

EPS = 1e-6
LANES = 128
HEAD_DIM = 64
POOL_WINDOWS = (2, 4, 8, 16)
POOL_HALO = 16
MOBA_BLOCK = 256
MOBA_TOPK = 3
ROPE_THETA = 10000.0
XATTN_HEADS = 4
CONV_WIDTH = 3
CONV_HALO = 8
FF_CHUNK = 256
NEG = -1e30
VMEM_LIMIT = 56 * 1024 * 1024

NT_DIMS = (((1,), (1,)), ((), ()))
TN_DIMS = (((0,), (0,)), ((), ()))


def _rms(x, g):
    return x * lax.rsqrt(jnp.mean(x * x, axis=-1, keepdims=True) + EPS) * g


def _head_norm_rope(y, g, cos, sin_signed):
    lane = lax.broadcasted_iota(jnp.int32, (1, LANES), 1)
    first = lane < HEAD_DIM
    sq = y * y
    s0 = jnp.sum(jnp.where(first, sq, 0.0), axis=-1, keepdims=True)
    s1 = jnp.sum(jnp.where(first, 0.0, sq), axis=-1, keepdims=True)
    r = lax.rsqrt(jnp.where(first, s0, s1) * (1.0 / HEAD_DIM) + EPS)
    yn = y * r * g
    low_half = (lane & (HEAD_DIM - 1)) < (HEAD_DIM // 2)
    rot = jnp.where(low_half,
                    pltpu.roll(yn, LANES - HEAD_DIM // 2, axis=1),
                    pltpu.roll(yn, HEAD_DIM // 2, axis=1))
    return yn * cos + rot * sin_signed


def _in_proj_kernel(x_ref, g_ref, w_ref, pw_ref, ps_ref, qg_ref, kg_ref, cos_ref, sin_ref,
                    pool_ref, q_ref, k_ref, v_ref, kmean_ref, ubuf_ref, *, tm, pool_width, attn_width):
    s = pl.program_id(1)
    hb = _rms(x_ref[0], g_ref[...]).astype(BF16)

    @pl.when(s == 0)
    def _():
        ubuf_ref[0:POOL_HALO, :] = jnp.zeros((POOL_HALO, pool_width), F32)

    u = jnp.dot(hb, w_ref[:, 0:pool_width], preferred_element_type=F32)
    ubuf_ref[POOL_HALO:POOL_HALO + tm, :] = u
    t = s * tm + lax.broadcasted_iota(jnp.int32, (tm, 1), 0)
    gd = pool_width // len(POOL_WINDOWS)
    for g, w in enumerate(POOL_WINDOWS):
        lanes = slice(g * gd, (g + 1) * gd)
        ug = u[:, lanes]
        acc = ug
        for d in range(1, w):
            acc = acc + ubuf_ref[POOL_HALO - d:POOL_HALO - d + tm, lanes]
        cnt = jnp.minimum(t + 1, w).astype(F32)
        pooled = acc / cnt - ug
        mixed = jnp.dot(pooled.astype(BF16), pw_ref[g], preferred_element_type=F32)
        pool_ref[0, :, lanes] = (mixed * ps_ref[:, lanes]).astype(BF16)
    ubuf_ref[0:POOL_HALO, :] = ubuf_ref[tm:tm + POOL_HALO, :]

    cos = cos_ref[...]
    sin = sin_ref[...]
    q0, k0, v0 = pool_width, pool_width + attn_width, pool_width + 2 * attn_width
    for c in range(attn_width // LANES):
        lanes = slice(c * LANES, (c + 1) * LANES)
        yq = jnp.dot(hb, w_ref[:, q0 + c * LANES:q0 + (c + 1) * LANES], preferred_element_type=F32)
        q_ref[0, :, lanes] = _head_norm_rope(yq, qg_ref[...], cos, sin).astype(BF16)
        yk = jnp.dot(hb, w_ref[:, k0 + c * LANES:k0 + (c + 1) * LANES], preferred_element_type=F32)
        kr = _head_norm_rope(yk, kg_ref[...], cos, sin)
        k_ref[0, :, lanes] = kr.astype(BF16)
        for r in range(tm // MOBA_BLOCK):
            km = jnp.sum(kr[r * MOBA_BLOCK:(r + 1) * MOBA_BLOCK], axis=0, keepdims=True) * (1.0 / MOBA_BLOCK)
            kmean_ref[0, r, :, lanes] = km
    v_ref[0] = jnp.dot(hb, w_ref[:, v0:v0 + attn_width], preferred_element_type=F32).astype(BF16)


def _in_proj(x, g, w_in, pool_w, pool_scale, qg, kg, cos, sin, *, tm):
    B, S, D = x.shape
    n_groups, gd, _ = pool_w.shape
    pool_width = n_groups * gd
    attn_width = (w_in.shape[1] - pool_width) // 3
    nblk = S // MOBA_BLOCK
    kern = functools.partial(_in_proj_kernel, tm=tm, pool_width=pool_width, attn_width=attn_width)
    const2 = lambda b, s: (0, 0)
    tile3 = lambda b, s: (b, s, 0)
    return pl.pallas_call(
        kern,
        grid=(B, S // tm),
        in_specs=[
            pl.BlockSpec((1, tm, D), tile3),
            pl.BlockSpec((1, D), const2),
            pl.BlockSpec(w_in.shape, const2),
            pl.BlockSpec(pool_w.shape, lambda b, s: (0, 0, 0)),
            pl.BlockSpec((1, pool_width), const2),
            pl.BlockSpec((1, LANES), const2),
            pl.BlockSpec((1, LANES), const2),
            pl.BlockSpec((tm, LANES), lambda b, s: (s, 0)),
            pl.BlockSpec((tm, LANES), lambda b, s: (s, 0)),
        ],
        out_specs=[
            pl.BlockSpec((1, tm, pool_width), tile3),
            pl.BlockSpec((1, tm, attn_width), tile3),
            pl.BlockSpec((1, tm, attn_width), tile3),
            pl.BlockSpec((1, tm, attn_width), tile3),
            pl.BlockSpec((1, tm // MOBA_BLOCK, 1, attn_width), lambda b, s: (b, s, 0, 0)),
        ],
        out_shape=[
            jax.ShapeDtypeStruct((B, S, pool_width), BF16),
            jax.ShapeDtypeStruct((B, S, attn_width), BF16),
            jax.ShapeDtypeStruct((B, S, attn_width), BF16),
            jax.ShapeDtypeStruct((B, S, attn_width), BF16),
            jax.ShapeDtypeStruct((B, nblk, 1, attn_width), F32),
        ],
        scratch_shapes=[pltpu.VMEM((POOL_HALO + tm, pool_width), F32)],
        compiler_params=pltpu.CompilerParams(
            dimension_semantics=("arbitrary", "arbitrary"), vmem_limit_bytes=VMEM_LIMIT),
        name="in_proj",
    )(x, g, w_in, pool_w, pool_scale, qg, kg, cos, sin)


def _moba_kernel(q_ref, k_ref, v_ref, kmean_ref, o_ref, sel_ref, acc_ref, *, nblk):
    i = pl.program_id(2)
    blk = MOBA_BLOCK
    lane = lax.broadcasted_iota(jnp.int32, (1, LANES), 1)
    qs = q_ref[0] * jnp.asarray(HEAD_DIM ** -0.5, BF16)
    zero = jnp.zeros_like(qs)
    qh = (jnp.where(lane < HEAD_DIM, qs, zero), jnp.where(lane < HEAD_DIM, zero, qs))

    km = kmean_ref[0]
    km_hi = km.astype(BF16)
    km_lo = (km - km_hi.astype(F32)).astype(BF16)
    bidx = lax.broadcasted_iota(jnp.int32, (nblk, 1), 0)
    eligible = bidx < i
    for h in range(2):
        gate = (lax.dot_general(km_hi, qh[h], NT_DIMS, preferred_element_type=F32)
                + lax.dot_general(km_lo, qh[h], NT_DIMS, preferred_element_type=F32))
        gate = jnp.where(eligible, gate, -jnp.inf)
        beaten = jnp.zeros((nblk, blk), F32)
        for j in range(nblk):
            row = gate[j:j + 1, :]
            wins = (row > gate) | ((row == gate) & (bidx > j))
            beaten = beaten + jnp.where(wins, 1.0, 0.0)
        sel = jnp.where((beaten < MOBA_TOPK) & eligible, 1.0, 0.0)
        for j in range(nblk):
            sel_ref[h, j] = sel[j:j + 1, :]

    acc_ref[...] = jnp.zeros_like(acc_ref)

    def attend(kb, vb, valid, carry):
        new = []
        for h in range(2):
            m, l = carry[2 * h], carry[2 * h + 1]
            st = lax.dot_general(kb, qh[h], NT_DIMS, preferred_element_type=F32)
            st = jnp.where(valid[h], st, NEG)
            m_new = jnp.maximum(m, jnp.max(st, axis=0, keepdims=True))
            alpha = jnp.exp(m - m_new)
            p = jnp.exp(st - m_new)
            l_new = alpha * l + jnp.sum(p, axis=0, keepdims=True)
            pv = lax.dot_general(vb, p.astype(BF16), TN_DIMS, preferred_element_type=F32)
            rows = slice(h * HEAD_DIM, (h + 1) * HEAD_DIM)
            acc_ref[h] = alpha * acc_ref[h] + pv[rows]
            new += [m_new, l_new]
        return tuple(new)

    def past_block(j, carry):
        off = pl.multiple_of(j * blk, blk)
        valid = [sel_ref[h, j] > 0.0 for h in range(2)]
        return attend(k_ref[0, pl.ds(off, blk), :], v_ref[0, pl.ds(off, blk), :], valid, carry)

    init = (jnp.full((1, blk), NEG, F32), jnp.zeros((1, blk), F32)) * 2
    carry = lax.fori_loop(0, i, past_block, init)

    off = pl.multiple_of(i * blk, blk)
    causal = (lax.broadcasted_iota(jnp.int32, (blk, blk), 0) <= lax.broadcasted_iota(jnp.int32, (blk, blk), 1))
    carry = attend(k_ref[0, pl.ds(off, blk), :], v_ref[0, pl.ds(off, blk), :], [causal, causal], carry)

    out_t = jnp.concatenate([acc_ref[0] / carry[1], acc_ref[1] / carry[3]], axis=0)
    o_ref[0] = out_t.T.astype(BF16)


def _moba(q, k, v, kmean):
    B, S, W = q.shape
    nblk = S // MOBA_BLOCK
    kern = functools.partial(_moba_kernel, nblk=nblk)
    return pl.pallas_call(
        kern,
        grid=(B, W // LANES, nblk),
        in_specs=[
            pl.BlockSpec((1, MOBA_BLOCK, LANES), lambda b, p, i: (b, i, p)),
            pl.BlockSpec((1, S, LANES), lambda b, p, i: (b, 0, p)),
            pl.BlockSpec((1, S, LANES), lambda b, p, i: (b, 0, p)),
            pl.BlockSpec((1, nblk, LANES), lambda b, p, i: (b, 0, p)),
        ],
        out_specs=pl.BlockSpec((1, MOBA_BLOCK, LANES), lambda b, p, i: (b, i, p)),
        out_shape=jax.ShapeDtypeStruct((B, S, W), BF16),
        scratch_shapes=[pltpu.VMEM((2, nblk, 1, MOBA_BLOCK), F32),
                        pltpu.VMEM((2, HEAD_DIM, MOBA_BLOCK), F32)],
        compiler_params=pltpu.CompilerParams(
            dimension_semantics=("arbitrary", "arbitrary", "arbitrary"), vmem_limit_bytes=VMEM_LIMIT),
        name="moba",
    )(q, k, v, kmean)


def _mem_kv_kernel(mem_ref, g_ref, w_ref, kg_ref, k_ref, v_ref, *, d_model):
    mb = _rms(mem_ref[0], g_ref[...]).astype(BF16)
    hd = d_model // XATTN_HEADS
    for h in range(XATTN_HEADS):
        lanes = slice(h * hd, (h + 1) * hd)
        kh = jnp.dot(mb, w_ref[:, lanes], preferred_element_type=F32)
        k_ref[0, :, lanes] = _rms(kh, kg_ref[...]).astype(BF16)
    v_ref[0] = jnp.dot(mb, w_ref[:, d_model:2 * d_model], preferred_element_type=F32).astype(BF16)


def _mem_kv(mem, g, w_xkv, kg):
    B, M, D = mem.shape
    const2 = lambda b: (0, 0)
    blk3 = lambda b: (b, 0, 0)
    return pl.pallas_call(
        functools.partial(_mem_kv_kernel, d_model=D),
        grid=(B,),
        in_specs=[pl.BlockSpec((1, M, D), blk3), pl.BlockSpec((1, D), const2),
                  pl.BlockSpec(w_xkv.shape, const2), pl.BlockSpec((1, D // XATTN_HEADS), const2)],
        out_specs=[pl.BlockSpec((1, M, D), blk3), pl.BlockSpec((1, M, D), blk3)],
        out_shape=[jax.ShapeDtypeStruct((B, M, D), BF16)] * 2,
        compiler_params=pltpu.CompilerParams(
            dimension_semantics=("arbitrary",), vmem_limit_bytes=VMEM_LIMIT),
        name="mem_kv",
    )(mem, g, w_xkv, kg)


def _mix_xattn_kernel(x_ref, pool_ref, attn_ref, wo_ref, g_ref, wq_ref, qg_ref, k_ref, v_ref, wxo_ref,
                      o_ref, oh_ref, *, pool_width, d_model):
    x1 = (x_ref[0]
          + jnp.dot(pool_ref[0], wo_ref[0:pool_width, :], preferred_element_type=F32)
          + jnp.dot(attn_ref[0], wo_ref[pool_width:, :], preferred_element_type=F32))
    hb = _rms(x1, g_ref[...]).astype(BF16)
    hd = d_model // XATTN_HEADS
    for h in range(XATTN_HEADS):
        lanes = slice(h * hd, (h + 1) * hd)
        qh = jnp.dot(hb, wq_ref[:, lanes], preferred_element_type=F32)
        qh = (_rms(qh, qg_ref[...]) * (hd ** -0.5)).astype(BF16)
        s = lax.dot_general(qh, k_ref[0, :, lanes], NT_DIMS, preferred_element_type=F32)
        p = jnp.exp(s - jnp.max(s, axis=-1, keepdims=True))
        l = jnp.sum(p, axis=-1, keepdims=True)
        oh = jnp.dot(p.astype(BF16), v_ref[0, :, lanes], preferred_element_type=F32) / l
        oh_ref[:, lanes] = oh.astype(BF16)
    o_ref[0] = x1 + jnp.dot(oh_ref[...], wxo_ref[...], preferred_element_type=F32)


def _mix_xattn(x, pool, attn, w_out, g, w_xq, qg, kx, vx, w_xo, *, tm):
    B, S, D = x.shape
    M = kx.shape[1]
    pool_width = pool.shape[2]
    const2 = lambda b, s: (0, 0)
    tile3 = lambda b, s: (b, s, 0)
    mem3 = lambda b, s: (b, 0, 0)
    return pl.pallas_call(
        functools.partial(_mix_xattn_kernel, pool_width=pool_width, d_model=D),
        grid=(B, S // tm),
        in_specs=[
            pl.BlockSpec((1, tm, D), tile3),
            pl.BlockSpec((1, tm, pool_width), tile3),
            pl.BlockSpec((1, tm, attn.shape[2]), tile3),
            pl.BlockSpec(w_out.shape, const2),
            pl.BlockSpec((1, D), const2),
            pl.BlockSpec(w_xq.shape, const2),
            pl.BlockSpec((1, D // XATTN_HEADS), const2),
            pl.BlockSpec((1, M, D), mem3),
            pl.BlockSpec((1, M, D), mem3),
            pl.BlockSpec(w_xo.shape, const2),
        ],
        out_specs=pl.BlockSpec((1, tm, D), tile3),
        out_shape=jax.ShapeDtypeStruct((B, S, D), F32),
        scratch_shapes=[pltpu.VMEM((tm, D), BF16)],
        compiler_params=pltpu.CompilerParams(
            dimension_semantics=("arbitrary", "arbitrary"), vmem_limit_bytes=VMEM_LIMIT),
        name="mix_xattn",
    )(x, pool, attn, w_out, g, w_xq, qg, kx, vx, w_xo)


def _conv_ffn_kernel(x_ref, g_ref, wu_ref, cw_ref, cb_ref, wd_ref, o_ref,
                     hb_ref, halo_ref, buf_ref, acc_ref, *, tm, n_chunks):
    s = pl.program_id(1)

    @pl.when(s == 0)
    def _():
        halo_ref[...] = jnp.zeros_like(halo_ref)

    x = x_ref[0]
    hb_ref[...] = _rms(x, g_ref[...]).astype(BF16)
    acc_ref[...] = x

    def conv(c):
        up = jnp.dot(hb_ref[...], wu_ref[c], preferred_element_type=F32)
        buf_ref[0:CONV_HALO, :] = halo_ref[c]
        buf_ref[CONV_HALO:CONV_HALO + tm, :] = up
        halo_ref[c] = up[tm - CONV_HALO:, :]
        cw = cw_ref[c]
        y = up * cw[CONV_WIDTH - 1:CONV_WIDTH, :] + cb_ref[c]
        for d in range(1, CONV_WIDTH):
            y = y + buf_ref[CONV_HALO - d:CONV_HALO - d + tm, :] * cw[CONV_WIDTH - 1 - d:CONV_WIDTH - d, :]
        return y

    def chunk(c, _):
        gate = conv(c)
        val = conv(c + n_chunks)
        act = (gate * (1.0 / (1.0 + jnp.exp(-gate))) * val).astype(BF16)
        acc_ref[...] += jnp.dot(act, wd_ref[c], preferred_element_type=F32)
        return 0

    lax.fori_loop(0, n_chunks, chunk, 0)
    o_ref[0] = acc_ref[...]


def _conv_ffn(x, g, w_up_c, conv_w_c, conv_b_c, w_down_c, *, tm):
    B, S, D = x.shape
    n_chunks = w_down_c.shape[0]
    const2 = lambda b, s: (0, 0)
    const3 = lambda b, s: (0, 0, 0)
    tile3 = lambda b, s: (b, s, 0)
    return pl.pallas_call(
        functools.partial(_conv_ffn_kernel, tm=tm, n_chunks=n_chunks),
        grid=(B, S // tm),
        in_specs=[
            pl.BlockSpec((1, tm, D), tile3),
            pl.BlockSpec((1, D), const2),
            pl.BlockSpec(w_up_c.shape, const3),
            pl.BlockSpec(conv_w_c.shape, const3),
            pl.BlockSpec(conv_b_c.shape, const3),
            pl.BlockSpec(w_down_c.shape, const3),
        ],
        out_specs=pl.BlockSpec((1, tm, D), tile3),
        out_shape=jax.ShapeDtypeStruct((B, S, D), F32),
        scratch_shapes=[
            pltpu.VMEM((tm, D), BF16),
            pltpu.VMEM((2 * n_chunks, CONV_HALO, FF_CHUNK), F32),
            pltpu.VMEM((CONV_HALO + tm, FF_CHUNK), F32),
            pltpu.VMEM((tm, D), F32),
        ],
        compiler_params=pltpu.CompilerParams(
            dimension_semantics=("arbitrary", "arbitrary"), vmem_limit_bytes=VMEM_LIMIT),
        name="conv_ffn",
    )(x, g, w_up_c, conv_w_c, conv_b_c, w_down_c)


def _rope_tables(S):
    half = HEAD_DIM // 2
    inv_freq = ROPE_THETA ** (-jnp.arange(half, dtype=F32) / half)
    ang = jnp.arange(S).astype(F32)[:, None] * inv_freq[None, :]
    cos, sin = jnp.cos(ang), jnp.sin(ang)
    reps = LANES // HEAD_DIM
    return (jnp.tile(jnp.concatenate([cos, cos], axis=-1), (1, reps)),
            jnp.tile(jnp.concatenate([-sin, sin], axis=-1), (1, reps)))


def kernel(x, mem, norm_mix_g, w_in, pool_w, pool_scale, q_norm_g, k_norm_g, w_out, norm_xattn_g, norm_mem_g, w_xq, w_xkv, xq_norm_g, xk_norm_g, w_xo, norm_ffn_g, w_up, conv_w, conv_b, w_down):
    B, S, D = x.shape
    depth = w_in.shape[0]
    d_ff = w_down.shape[1]
    assert S % 512 == 0 and d_ff % FF_CHUNK == 0
    n_chunks = d_ff // FF_CHUNK
    cos, sin = _rope_tables(S)
    row = lambda a: a.reshape(1, -1)
    two_heads = lambda a: jnp.tile(a, LANES // HEAD_DIM).reshape(1, LANES)
    for l in range(depth):
        pool, q, k, v, kmean = _in_proj(
            x, row(norm_mix_g[l]), w_in[l].astype(BF16), pool_w[l].astype(BF16), row(pool_scale[l]),
            two_heads(q_norm_g[l]), two_heads(k_norm_g[l]), cos, sin, tm=512)
        attn = _moba(q, k, v, kmean.reshape(B, S // MOBA_BLOCK, -1))
        kx, vx = _mem_kv(mem, row(norm_mem_g[l]), w_xkv[l].astype(BF16), row(xk_norm_g[l]))
        x = _mix_xattn(x, pool, attn, w_out[l].astype(BF16), row(norm_xattn_g[l]), w_xq[l].astype(BF16),
                       row(xq_norm_g[l]), kx, vx, w_xo[l].astype(BF16), tm=512)
        w_up_c = w_up[l].astype(BF16).reshape(D, 2 * n_chunks, FF_CHUNK).transpose(1, 0, 2)
        conv_w_c = conv_w[l].reshape(CONV_WIDTH, 2 * n_chunks, FF_CHUNK).transpose(1, 0, 2)
        conv_b_c = conv_b[l].reshape(2 * n_chunks, 1, FF_CHUNK)
        w_down_c = w_down[l].astype(BF16).reshape(n_chunks, FF_CHUNK, D)
        x = _conv_ffn(x, row(norm_ffn_g[l]), w_up_c, conv_w_c, conv_b_c, w_down_c, tm=512)
    return x
```

```python
import functools

import jax
import jax.numpy as jnp
from jax import lax
from jax.experimental import pallas as pl
from jax.experimental.pallas import tpu as pltpu

F32 = jnp.float32
BF16 = jnp.bfloat16

EPS = 1e-6
LANES = 128
HEAD_DIM = 64
POOL_WINDOWS = (2, 4, 8, 16)
POOL_HALO = 16
MOBA_BLOCK = 256
MOBA_TOPK = 3
ROPE_THETA = 10000.0
XATTN_HEADS = 4
CONV_WIDTH = 3
CONV_HALO = 8
FF_CHUNK = 256
NEG = -1e30
VMEM_LIMIT = 56 * 1024 * 1024

NT_DIMS = (((1,), (1,)), ((), ()))
TN_DIMS = (((0,), (0,)), ((), ()))


def _rms(x, g):
    return x * lax.rsqrt(jnp.mean(x * x, axis=-1, keepdims=True) + EPS) * g


def _head_norm_rope(y, g, cos, sin_signed):
    lane = lax.broadcasted_iota(jnp.int32, (1, LANES), 1)
    first = lane < HEAD_DIM
    sq = y * y
    s0 = jnp.sum(jnp.where(first, sq, 0.0), axis=-1, keepdims=True)
    s1 = jnp.sum(jnp.where(first, 0.0, sq), axis=-1, keepdims=True)
    r = lax.rsqrt(jnp.where(first, s0, s1) * (1.0 / HEAD_DIM) + EPS)
    yn = y * r * g
    low_half = (lane & (HEAD_DIM - 1)) < (HEAD_DIM // 2)
    rot = jnp.where(low_half,
                    pltpu.roll(yn, LANES - HEAD_DIM // 2, axis=1),
                    pltpu.roll(yn, HEAD_DIM // 2, axis=1))
    return yn * cos + rot * sin_signed


def _in_proj_kernel(x_ref, g_ref, w_ref, pw_ref, ps_ref, qg_ref, kg_ref, cos_ref, sin_ref,
                    pool_ref, q_ref, k_ref, v_ref, kmean_ref, ubuf_ref, *, tm, pool_width, attn_width):
    s = pl.program_id(1)
    hb = _rms(x_ref[0], g_ref[...]).astype(BF16)

    @pl.when(s == 0)
    def _():
        ubuf_ref[0:POOL_HALO, :] = jnp.zeros((POOL_HALO, pool_width), F32)

    u = jnp.dot(hb, w_ref[:, 0:pool_width], preferred_element_type=F32)
    ubuf_ref[POOL_HALO:POOL_HALO + tm, :] = u
    t = s * tm + lax.broadcasted_iota(jnp.int32, (tm, 1), 0)
    gd = pool_width // len(POOL_WINDOWS)
    for g, w in enumerate(POOL_WINDOWS):
        lanes = slice(g * gd, (g + 1) * gd)
        ug = u[:, lanes]
        acc = ug
        for d in range(1, w):
            acc = acc + ubuf_ref[POOL_HALO - d:POOL_HALO - d + tm, lanes]
        cnt = jnp.minimum(t + 1, w).astype(F32)
        pooled = acc / cnt - ug
        mixed = jnp.dot(pooled.astype(BF16), pw_ref[g], preferred_element_type=F32)
        pool_ref[0, :, lanes] = (mixed * ps_ref[:, lanes]).astype(BF16)
    ubuf_ref[0:POOL_HALO, :] = ubuf_ref[tm:tm + POOL_HALO, :]

    cos = cos_ref[...]
    sin = sin_ref[...]
    q0, k0, v0 = pool_width, pool_width + attn_width, pool_width + 2 * attn_width
    for c in range(attn_width // LANES):
        lanes = slice(c * LANES, (c + 1) * LANES)
        yq = jnp.dot(hb, w_ref[:, q0 + c * LANES:q0 + (c + 1) * LANES], preferred_element_type=F32)
        q_ref[0, :, lanes] = _head_norm_rope(yq, qg_ref[...], cos, sin).astype(BF16)
        yk = jnp.dot(hb, w_ref[:, k0 + c * LANES:k0 + (c + 1) * LANES], preferred_element_type=F32)
        kr = _head_norm_rope(yk, kg_ref[...], cos, sin)
        k_ref[0, :, lanes] = kr.astype(BF16)
        for r in range(tm // MOBA_BLOCK):
            km = jnp.sum(kr[r * MOBA_BLOCK:(r + 1) * MOBA_BLOCK], axis=0, keepdims=True) * (1.0 / MOBA_BLOCK)
            kmean_ref[0, r, :, lanes] = km
    v_ref[0] = jnp.dot(hb, w_ref[:, v0:v0 + attn_width], preferred_element_type=F32).astype(BF16)


def _in_proj(x, g, w_in, pool_w, pool_scale, qg, kg, cos, sin, *, tm):
    B, S, D = x.shape
    n_groups, gd, _ = pool_w.shape
    pool_width = n_groups * gd
    attn_width = (w_in.shape[1] - pool_width) // 3
    nblk = S // MOBA_BLOCK
    kern = functools.partial(_in_proj_kernel, tm=tm, pool_width=pool_width, attn_width=attn_width)
    const2 = lambda b, s: (0, 0)
    tile3 = lambda b, s: (b, s, 0)
    return pl.pallas_call(
        kern,
        grid=(B, S // tm),
        in_specs=[
            pl.BlockSpec((1, tm, D), tile3),
            pl.BlockSpec((1, D), const2),
            pl.BlockSpec(w_in.shape, const2),
            pl.BlockSpec(pool_w.shape, lambda b, s: (0, 0, 0)),
            pl.BlockSpec((1, pool_width), const2),
            pl.BlockSpec((1, LANES), const2),
            pl.BlockSpec((1, LANES), const2),
            pl.BlockSpec((tm, LANES), lambda b, s: (s, 0)),
            pl.BlockSpec((tm, LANES), lambda b, s: (s, 0)),
        ],
        out_specs=[
            pl.BlockSpec((1, tm, pool_width), tile3),
            pl.BlockSpec((1, tm, attn_width), tile3),
            pl.BlockSpec((1, tm, attn_width), tile3),
            pl.BlockSpec((1, tm, attn_width), tile3),
            pl.BlockSpec((1, tm // MOBA_BLOCK, 1, attn_width), lambda b, s: (b, s, 0, 0)),
        ],
        out_shape=[
            jax.ShapeDtypeStruct((B, S, pool_width), BF16),
            jax.ShapeDtypeStruct((B, S, attn_width), BF16),
            jax.ShapeDtypeStruct((B, S, attn_width), BF16),
            jax.ShapeDtypeStruct((B, S, attn_width), BF16),
            jax.ShapeDtypeStruct((B, nblk, 1, attn_width), F32),
        ],
        scratch_shapes=[pltpu.VMEM((POOL_HALO + tm, pool_width), F32)],
        compiler_params=pltpu.CompilerParams(
            dimension_semantics=("arbitrary", "arbitrary"), vmem_limit_bytes=VMEM_LIMIT),
        name="in_proj",
    )(x, g, w_in, pool_w, pool_scale, qg, kg, cos, sin)


def _moba_kernel(q_ref, k_ref, v_ref, kmean_ref, o_ref, qh_ref, sel_ref, s_ref, m_ref, l_ref, acc_ref, *, nblk, npairs):
    i = pl.program_id(2)
    blk = MOBA_BLOCK
    lane = lax.broadcasted_iota(jnp.int32, (1, LANES), 1)
    bidx = lax.broadcasted_iota(jnp.int32, (nblk, 1), 0)
    eligible = bidx < i

    for pr in range(npairs):
        lanes = slice(pr * LANES, (pr + 1) * LANES)
        qs = q_ref[0, :, lanes] * jnp.asarray(HEAD_DIM ** -0.5, BF16)
        zero = jnp.zeros_like(qs)
        km = kmean_ref[0, :, lanes]
        km_hi = km.astype(BF16)
        km_lo = (km - km_hi.astype(F32)).astype(BF16)
        for hh in range(2):
            h = 2 * pr + hh
            qm = jnp.where(lane < HEAD_DIM, qs, zero) if hh == 0 else jnp.where(lane < HEAD_DIM, zero, qs)
            qh_ref[h] = qm
            gate = (lax.dot_general(km_hi, qm, NT_DIMS, preferred_element_type=F32)
                    + lax.dot_general(km_lo, qm, NT_DIMS, preferred_element_type=F32))
            gate = jnp.where(eligible, gate, -jnp.inf)
            beaten = jnp.zeros((nblk, blk), F32)
            for j in range(nblk):
                row = gate[j:j + 1, :]
                wins = (row > gate) | ((row == gate) & (bidx > j))
                beaten = beaten + jnp.where(wins, 1.0, 0.0)
            sel = jnp.where((beaten < MOBA_TOPK) & eligible, 1.0, 0.0)
            sel_ref[h, 0] = jnp.zeros((1, blk), F32)
            for j in range(nblk):
                sel_ref[h, j + 1] = sel[j:j + 1, :]

    m_ref[...] = jnp.full(m_ref.shape, NEG, F32)
    l_ref[...] = jnp.zeros_like(l_ref)
    acc_ref[...] = jnp.zeros_like(acc_ref)

    def scores(slot, kblock):
        off = pl.multiple_of(kblock * blk, blk)
        for pr in range(npairs):
            kb = k_ref[0, pl.ds(off, blk), pr * LANES:(pr + 1) * LANES]
            for hh in range(2):
                h = 2 * pr + hh
                s_ref[slot, h] = lax.dot_general(kb, qh_ref[h], NT_DIMS, preferred_element_type=F32)

    def accumulate(slot, kblock, sel_row):
        off = pl.multiple_of(kblock * blk, blk)
        for pr in range(npairs):
            vb = v_ref[0, pl.ds(off, blk), pr * LANES:(pr + 1) * LANES]
            for hh in range(2):
                h = 2 * pr + hh
                st = s_ref[slot, h]
                m = m_ref[h]
                if sel_row is None:
                    causal = (lax.broadcasted_iota(jnp.int32, (blk, blk), 0)
                              <= lax.broadcasted_iota(jnp.int32, (blk, blk), 1))
                    st = jnp.where(causal, st, NEG)
                    m_new = jnp.maximum(m, jnp.max(st, axis=0, keepdims=True))
                    p = jnp.exp(st - m_new)
                else:
                    picked = sel_ref[h, sel_row] > 0.0
                    m_new = jnp.maximum(m, jnp.where(picked, jnp.max(st, axis=0, keepdims=True), NEG))
                    p = jnp.exp(st - jnp.where(picked, m_new, -NEG))
                alpha = jnp.exp(m - m_new)
                m_ref[h] = m_new
                l_ref[h] = alpha * l_ref[h] + jnp.sum(p, axis=0, keepdims=True)
                pv = lax.dot_general(vb, p.astype(BF16), TN_DIMS, preferred_element_type=F32)
                acc_ref[h] = alpha * acc_ref[h] + pv[hh * HEAD_DIM:(hh + 1) * HEAD_DIM]

    pad = (i + 1) & 1
    n_pairs = (i + 1 + pad) // 2
    scores(0, 0)

    def pair(t, _):
        a = 2 * t - pad
        scores(1, a + 1)
        accumulate(0, jnp.maximum(a, 0), a + 1)
        scores(0, a + 2)
        accumulate(1, a + 1, a + 2)
        return 0

    lax.fori_loop(0, n_pairs - 1, pair, 0)
    scores(1, i)
    accumulate(0, jnp.maximum(i - 1, 0), i)
    accumulate(1, i, None)

    for pr in range(npairs):
        out_t = jnp.concatenate([acc_ref[2 * pr] / l_ref[2 * pr], acc_ref[2 * pr + 1] / l_ref[2 * pr + 1]], axis=0)
        o_ref[0, :, pr * LANES:(pr + 1) * LANES] = out_t.T.astype(BF16)


def _moba(q, k, v, kmean, *, npairs):
    B, S, W = q.shape
    nblk = S // MOBA_BLOCK
    wstep = npairs * LANES
    kern = functools.partial(_moba_kernel, nblk=nblk, npairs=npairs)
    return pl.pallas_call(
        kern,
        grid=(B, W // wstep, nblk),
        in_specs=[
            pl.BlockSpec((1, MOBA_BLOCK, wstep), lambda b, g, i: (b, i, g)),
            pl.BlockSpec((1, S, wstep), lambda b, g, i: (b, 0, g)),
            pl.BlockSpec((1, S, wstep), lambda b, g, i: (b, 0, g)),
            pl.BlockSpec((1, nblk, wstep), lambda b, g, i: (b, 0, g)),
        ],
        out_specs=pl.BlockSpec((1, MOBA_BLOCK, wstep), lambda b, g, i: (b, i, g)),
        out_shape=jax.ShapeDtypeStruct((B, S, W), BF16),
        scratch_shapes=[pltpu.VMEM((2 * npairs, MOBA_BLOCK, LANES), BF16),
                        pltpu.VMEM((2 * npairs, nblk + 1, 1, MOBA_BLOCK), F32),
                        pltpu.VMEM((2, 2 * npairs, MOBA_BLOCK, MOBA_BLOCK), F32),
                        pltpu.VMEM((2 * npairs, 1, MOBA_BLOCK), F32),
                        pltpu.VMEM((2 * npairs, 1, MOBA_BLOCK), F32),
                        pltpu.VMEM((2 * npairs, HEAD_DIM, MOBA_BLOCK), F32)],
        compiler_params=pltpu.CompilerParams(
            dimension_semantics=("arbitrary", "arbitrary", "arbitrary"), vmem_limit_bytes=VMEM_LIMIT),
        name="moba",
    )(q, k, v, kmean)


def _mem_kv_kernel(mem_ref, g_ref, w_ref, kg_ref, k_ref, v_ref, *, d_model):
    mb = _rms(mem_ref[0], g_ref[...]).astype(BF16)
    hd = d_model // XATTN_HEADS
    for h in range(XATTN_HEADS):
        lanes = slice(h * hd, (h + 1) * hd)
        kh = jnp.dot(mb, w_ref[:, lanes], preferred_element_type=F32)
        k_ref[0, :, lanes] = _rms(kh, kg_ref[...]).astype(BF16)
    v_ref[0] = jnp.dot(mb, w_ref[:, d_model:2 * d_model], preferred_element_type=F32).astype(BF16)


def _mem_kv(mem, g, w_xkv, kg):
    B, M, D = mem.shape
    const2 = lambda b: (0, 0)
    blk3 = lambda b: (b, 0, 0)
    return pl.pallas_call(
        functools.partial(_mem_kv_kernel, d_model=D),
        grid=(B,),
        in_specs=[pl.BlockSpec((1, M, D), blk3), pl.BlockSpec((1, D), const2),
                  pl.BlockSpec(w_xkv.shape, const2), pl.BlockSpec((1, D // XATTN_HEADS), const2)],
        out_specs=[pl.BlockSpec((1, M, D), blk3), pl.BlockSpec((1, M, D), blk3)],
        out_shape=[jax.ShapeDtypeStruct((B, M, D), BF16)] * 2,
        compiler_params=pltpu.CompilerParams(
            dimension_semantics=("arbitrary",), vmem_limit_bytes=VMEM_LIMIT),
        name="mem_kv",
    )(mem, g, w_xkv, kg)


def _mix_xattn_kernel(x_ref, pool_ref, attn_ref, wo_ref, g_ref, wq_ref, qg_ref, k_ref, v_ref, wxo_ref,
                      o_ref, oh_ref, *, pool_width, d_model):
    x1 = (x_ref[0]
          + jnp.dot(pool_ref[0], wo_ref[0:pool_width, :], preferred_element_type=F32)
          + jnp.dot(attn_ref[0], wo_ref[pool_width:, :], preferred_element_type=F32))
    hb = _rms(x1, g_ref[...]).astype(BF16)
    hd = d_model // XATTN_HEADS
    for h in range(XATTN_HEADS):
        lanes = slice(h * hd, (h + 1) * hd)
        qh = jnp.dot(hb, wq_ref[:, lanes], preferred_element_type=F32)
        qh = (_rms(qh, qg_ref[...]) * (hd ** -0.5)).astype(BF16)
        s = lax.dot_general(qh, k_ref[0, :, lanes], NT_DIMS, preferred_element_type=F32)
        p = jnp.exp(s - jnp.max(s, axis=-1, keepdims=True))
        l = jnp.sum(p, axis=-1, keepdims=True)
        oh = jnp.dot(p.astype(BF16), v_ref[0, :, lanes], preferred_element_type=F32) / l
        oh_ref[:, lanes] = oh.astype(BF16)
    o_ref[0] = x1 + jnp.dot(oh_ref[...], wxo_ref[...], preferred_element_type=F32)


def _mix_xattn(x, pool, attn, w_out, g, w_xq, qg, kx, vx, w_xo, *, tm):
    B, S, D = x.shape
    M = kx.shape[1]
    pool_width = pool.shape[2]
    const2 = lambda b, s: (0, 0)
    tile3 = lambda b, s: (b, s, 0)
    mem3 = lambda b, s: (b, 0, 0)
    return pl.pallas_call(
        functools.partial(_mix_xattn_kernel, pool_width=pool_width, d_model=D),
        grid=(B, S // tm),
        in_specs=[
            pl.BlockSpec((1, tm, D), tile3),
            pl.BlockSpec((1, tm, pool_width), tile3),
            pl.BlockSpec((1, tm, attn.shape[2]), tile3),
            pl.BlockSpec(w_out.shape, const2),
            pl.BlockSpec((1, D), const2),
            pl.BlockSpec(w_xq.shape, const2),
            pl.BlockSpec((1, D // XATTN_HEADS), const2),
            pl.BlockSpec((1, M, D), mem3),
            pl.BlockSpec((1, M, D), mem3),
            pl.BlockSpec(w_xo.shape, const2),
        ],
        out_specs=pl.BlockSpec((1, tm, D), tile3),
        out_shape=jax.ShapeDtypeStruct((B, S, D), F32),
        scratch_shapes=[pltpu.VMEM((tm, D), BF16)],
        compiler_params=pltpu.CompilerParams(
            dimension_semantics=("arbitrary", "arbitrary"), vmem_limit_bytes=VMEM_LIMIT),
        name="mix_xattn",
    )(x, pool, attn, w_out, g, w_xq, qg, kx, vx, w_xo)


def _conv_ffn_kernel(x_ref, g_ref, wu_ref, cw_ref, cb_ref, wd_ref, o_ref,
                     hb_ref, halo_ref, buf_ref, acc_ref, *, tm, n_chunks):
    s = pl.program_id(1)

    @pl.when(s == 0)
    def _():
        halo_ref[...] = jnp.zeros_like(halo_ref)

    x = x_ref[0]
    hb_ref[...] = _rms(x, g_ref[...]).astype(BF16)
    acc_ref[...] = x

    def conv(c):
        up = jnp.dot(hb_ref[...], wu_ref[c], preferred_element_type=F32)
        buf_ref[0:CONV_HALO, :] = halo_ref[c]
        buf_ref[CONV_HALO:CONV_HALO + tm, :] = up
        halo_ref[c] = up[tm - CONV_HALO:, :]
        cw = cw_ref[c]
        y = up * cw[CONV_WIDTH - 1:CONV_WIDTH, :] + cb_ref[c]
        for d in range(1, CONV_WIDTH):
            y = y + buf_ref[CONV_HALO - d:CONV_HALO - d + tm, :] * cw[CONV_WIDTH - 1 - d:CONV_WIDTH - d, :]
        return y

    def chunk(c, _):
        gate = conv(c)
        val = conv(c + n_chunks)
        act = (gate * (1.0 / (1.0 + jnp.exp(-gate))) * val).astype(BF16)
        acc_ref[...] += jnp.dot(act, wd_ref[c], preferred_element_type=F32)
        return 0

    lax.fori_loop(0, n_chunks, chunk, 0)
    o_ref[0] = acc_ref[...]


def _conv_ffn(x, g, w_up_c, conv_w_c, conv_b_c, w_down_c, *, tm):
    B, S, D = x.shape
    n_chunks = w_down_c.shape[0]
    const2 = lambda b, s: (0, 0)
    const3 = lambda b, s: (0, 0, 0)
    tile3 = lambda b, s: (b, s, 0)
    return pl.pallas_call(
        functools.partial(_conv_ffn_kernel, tm=tm, n_chunks=n_chunks),
        grid=(B, S // tm),
        in_specs=[
            pl.BlockSpec((1, tm, D), tile3),
            pl.BlockSpec((1, D), const2),
            pl.BlockSpec(w_up_c.shape, const3),
            pl.BlockSpec(conv_w_c.shape, const3),
            pl.BlockSpec(conv_b_c.shape, const3),
            pl.BlockSpec(w_down_c.shape, const3),
        ],
        out_specs=pl.BlockSpec((1, tm, D), tile3),
        out_shape=jax.ShapeDtypeStruct((B, S, D), F32),
        scratch_shapes=[
            pltpu.VMEM((tm, D), BF16),
            pltpu.VMEM((2 * n_chunks, CONV_HALO, FF_CHUNK), F32),
            pltpu.VMEM((CONV_HALO + tm, FF_CHUNK), F32),
            pltpu.VMEM((tm, D), F32),
        ],
        compiler_params=pltpu.CompilerParams(
            dimension_semantics=("arbitrary", "arbitrary"), vmem_limit_bytes=VMEM_LIMIT),
        name="conv_ffn",
    )(x, g, w_up_c, conv_w_c, conv_b_c, w_down_c)


def _rope_tables(S):
    half = HEAD_DIM // 2
    inv_freq = ROPE_THETA ** (-jnp.arange(half, dtype=F32) / half)
    ang = jnp.arange(S).astype(F32)[:, None] * inv_freq[None, :]
    cos, sin = jnp.cos(ang), jnp.sin(ang)
    reps = LANES // HEAD_DIM
    return (jnp.tile(jnp.concatenate([cos, cos], axis=-1), (1, reps)),
            jnp.tile(jnp.concatenate([-sin, sin], axis=-1), (1, reps)))


def kernel(x, mem, norm_mix_g, w_in, pool_w, pool_scale, q_norm_g, k_norm_g, w_out, norm_xattn_g, norm_mem_g, w_xq, w_xkv, xq_norm_g, xk_norm_g, w_xo, norm_ffn_g, w_up, conv_w, conv_b, w_down):
    B, S, D = x.shape
    depth = w_in.shape[0]
    d_ff = w_down.shape[1]
    assert S % 512 == 0 and d_ff % FF_CHUNK == 0
    n_chunks = d_ff // FF_CHUNK
    cos, sin = _rope_tables(S)
    row = lambda a: a.reshape(1, -1)
    two_heads = lambda a: jnp.tile(a, LANES // HEAD_DIM).reshape(1, LANES)
    for l in range(depth):
        pool, q, k, v, kmean = _in_proj(
            x, row(norm_mix_g[l]), w_in[l].astype(BF16), pool_w[l].astype(BF16), row(pool_scale[l]),
            two_heads(q_norm_g[l]), two_heads(k_norm_g[l]), cos, sin, tm=512)
        attn = _moba(q, k, v, kmean.reshape(B, S // MOBA_BLOCK, -1), npairs=2)
        kx, vx = _mem_kv(mem, row(norm_mem_g[l]), w_xkv[l].astype(BF16), row(xk_norm_g[l]))
        x = _mix_xattn(x, pool, attn, w_out[l].astype(BF16), row(norm_xattn_g[l]), w_xq[l].astype(BF16),
                       row(xq_norm_g[l]), kx, vx, w_xo[l].astype(BF16), tm=512)
        w_up_c = w_up[l].astype(BF16).reshape(D, 2 * n_chunks, FF_CHUNK).transpose(1, 0, 2)
        conv_w_c = conv_w[l].reshape(CONV_WIDTH, 2 * n_chunks, FF_CHUNK).transpose(1, 0, 2)
        conv_b_c = conv_b[l].reshape(2 * n_chunks, 1, FF_CHUNK)
        w_down_c = w_down[l].astype(BF16).reshape(n_chunks, FF_CHUNK, D)
        x = _conv_ffn(x, row(norm_ffn_g[l]), w_up_c, conv_w_c, conv_b_c, w_down_c, tm=512)
    return x
```

```python
import functools

import jax
import jax.numpy as jnp
from jax import lax
from jax.experimental import pallas as pl
from jax.experimental.pallas import tpu as pltpu

F32 = jnp.float32
BF16 = jnp.bfloat16

EPS = 1e-6
LANES = 128
SUBLANES = 8
MXU_COLS = 256
HEAD_DIM = 64
POOL_WINDOWS = (2, 4, 8, 16)
POOL_HALO = SUBLANES * len(POOL_WINDOWS)
MOBA_BLOCK = 256
MOBA_TOPK = 3
ROPE_THETA = 10000.0
XATTN_HEADS = 4
CONV_WIDTH = 3
CONV_HALO = 8
FF_CHUNK = 256
NEG = -1e30
VMEM_LIMIT = 56 * 1024 * 1024

NT_DIMS = (((1,), (1,)), ((), ()))
TN_DIMS = (((0,), (0,)), ((), ()))


def _rms(x, g):
    return x * lax.rsqrt(jnp.mean(x * x, axis=-1, keepdims=True) + EPS) * g


def _first_head_lanes():
    lane = lax.broadcasted_iota(jnp.int32, (1, LANES), 1)
    return (lane & (HEAD_DIM // 2)) == 0


def _head_norm_rope(y, g, cos, sin_signed):
    first = _first_head_lanes()
    sq = y * y
    s0 = jnp.sum(jnp.where(first, sq, 0.0), axis=-1, keepdims=True)
    s1 = jnp.sum(jnp.where(first, 0.0, sq), axis=-1, keepdims=True)
    r = lax.rsqrt(jnp.where(first, s0, s1) * (1.0 / HEAD_DIM) + EPS)
    yn = y * r * g
    return yn * cos + pltpu.roll(yn, LANES // 2, axis=1) * sin_signed


def _in_proj_kernel(x_ref, g_ref, w_ref, pw_ref, ps_ref, qg_ref, kg_ref, cos_ref, sin_ref,
                    pool_ref, q_ref, k_ref, v_ref, kmean_ref, ubuf_ref, lva_ref, lvb_ref,
                    *, tm, pool_width, attn_width):
    s = pl.program_id(1)
    hb = _rms(x_ref[0], g_ref[...]).astype(BF16)

    @pl.when(s == 0)
    def _():
        ubuf_ref[0:POOL_HALO, :] = jnp.zeros((POOL_HALO, pool_width), F32)

    u = jnp.dot(hb, w_ref[:, 0:pool_width], preferred_element_type=F32)
    end = POOL_HALO + tm
    ubuf_ref[POOL_HALO:end, :] = u
    gd = pool_width // len(POOL_WINDOWS)
    src = ubuf_ref
    for k in range(1, len(POOL_WINDOWS) + 1):
        dst = lva_ref if k % 2 else lvb_ref
        assert POOL_WINDOWS[k - 1] == 2 ** k
        shift, r0, l0 = 2 ** (k - 1), SUBLANES * k, (k - 1) * gd
        dst[r0:end, l0:] = src[r0:end, l0:] + src[r0 - shift:end - shift, l0:]
        src = dst
    t = s * tm + lax.broadcasted_iota(jnp.int32, (tm, 1), 0)
    for g, w in enumerate(POOL_WINDOWS):
        lanes = slice(g * gd, (g + 1) * gd)
        win = (lva_ref if (g + 1) % 2 else lvb_ref)[POOL_HALO:end, lanes]
        cnt = jnp.minimum(t + 1, w).astype(F32)
        pooled = win / cnt - u[:, lanes]
        mixed = jnp.dot(pooled.astype(BF16), pw_ref[g], preferred_element_type=F32)
        pool_ref[0, :, lanes] = (mixed * ps_ref[:, lanes]).astype(BF16)
    ubuf_ref[0:POOL_HALO, :] = ubuf_ref[tm:end, :]

    cos = cos_ref[...]
    sin = sin_ref[...]
    q0, k0, v0 = pool_width, pool_width + attn_width, pool_width + 2 * attn_width
    for c in range(attn_width // MXU_COLS):
        yq = jnp.dot(hb, w_ref[:, q0 + c * MXU_COLS:q0 + (c + 1) * MXU_COLS], preferred_element_type=F32)
        yk = jnp.dot(hb, w_ref[:, k0 + c * MXU_COLS:k0 + (c + 1) * MXU_COLS], preferred_element_type=F32)
        for half in range(MXU_COLS // LANES):
            lanes = slice(c * MXU_COLS + half * LANES, c * MXU_COLS + (half + 1) * LANES)
            sub = slice(half * LANES, (half + 1) * LANES)
            q_ref[0, :, lanes] = _head_norm_rope(yq[:, sub], qg_ref[...], cos, sin).astype(BF16)
            kr = _head_norm_rope(yk[:, sub], kg_ref[...], cos, sin)
            k_ref[0, :, lanes] = kr.astype(BF16)
            for r in range(tm // MOBA_BLOCK):
                km = jnp.sum(kr[r * MOBA_BLOCK:(r + 1) * MOBA_BLOCK], axis=0, keepdims=True) * (1.0 / MOBA_BLOCK)
                kmean_ref[0, r, :, lanes] = km
    v_ref[0] = jnp.dot(hb, w_ref[:, v0:v0 + attn_width], preferred_element_type=F32).astype(BF16)


def _in_proj(x, g, w_in, pool_w, pool_scale, qg, kg, cos, sin, *, tm):
    B, S, D = x.shape
    n_groups, gd, _ = pool_w.shape
    pool_width = n_groups * gd
    attn_width = (w_in.shape[1] - pool_width) // 3
    nblk = S // MOBA_BLOCK
    kern = functools.partial(_in_proj_kernel, tm=tm, pool_width=pool_width, attn_width=attn_width)
    const2 = lambda b, s: (0, 0)
    tile3 = lambda b, s: (b, s, 0)
    return pl.pallas_call(
        kern,
        grid=(B, S // tm),
        in_specs=[
            pl.BlockSpec((1, tm, D), tile3),
            pl.BlockSpec((1, D), const2),
            pl.BlockSpec(w_in.shape, const2),
            pl.BlockSpec(pool_w.shape, lambda b, s: (0, 0, 0)),
            pl.BlockSpec((1, pool_width), const2),
            pl.BlockSpec((1, LANES), const2),
            pl.BlockSpec((1, LANES), const2),
            pl.BlockSpec((tm, LANES), lambda b, s: (s, 0)),
            pl.BlockSpec((tm, LANES), lambda b, s: (s, 0)),
        ],
        out_specs=[
            pl.BlockSpec((1, tm, pool_width), tile3),
            pl.BlockSpec((1, tm, attn_width), tile3),
            pl.BlockSpec((1, tm, attn_width), tile3),
            pl.BlockSpec((1, tm, attn_width), tile3),
            pl.BlockSpec((1, tm // MOBA_BLOCK, 1, attn_width), lambda b, s: (b, s, 0, 0)),
        ],
        out_shape=[
            jax.ShapeDtypeStruct((B, S, pool_width), BF16),
            jax.ShapeDtypeStruct((B, S, attn_width), BF16),
            jax.ShapeDtypeStruct((B, S, attn_width), BF16),
            jax.ShapeDtypeStruct((B, S, attn_width), BF16),
            jax.ShapeDtypeStruct((B, nblk, 1, attn_width), F32),
        ],
        scratch_shapes=[pltpu.VMEM((POOL_HALO + tm, pool_width), F32)] * 3,
        compiler_params=pltpu.CompilerParams(
            dimension_semantics=("arbitrary", "arbitrary"), vmem_limit_bytes=VMEM_LIMIT),
        name="in_proj",
    )(x, g, w_in, pool_w, pool_scale, qg, kg, cos, sin)


def _moba_kernel(q_ref, k_ref, v_ref, kmean_ref, o_ref, qh_ref, sel_ref, s_ref, m_ref, l_ref, acc_ref, *, nblk, npairs):
    i = pl.program_id(2)
    blk = MOBA_BLOCK
    first = _first_head_lanes()
    bidx = lax.broadcasted_iota(jnp.int32, (nblk, 1), 0)
    eligible = bidx < i

    for pr in range(npairs):
        lanes = slice(pr * LANES, (pr + 1) * LANES)
        qs = q_ref[0, :, lanes] * jnp.asarray(HEAD_DIM ** -0.5, BF16)
        zero = jnp.zeros_like(qs)
        km = kmean_ref[0, :, lanes]
        km_hi = km.astype(BF16)
        km_lo = (km - km_hi.astype(F32)).astype(BF16)
        for hh in range(2):
            h = 2 * pr + hh
            qm = jnp.where(first, qs, zero) if hh == 0 else jnp.where(first, zero, qs)
            qh_ref[h] = qm
            gate = (lax.dot_general(km_hi, qm, NT_DIMS, preferred_element_type=F32)
                    + lax.dot_general(km_lo, qm, NT_DIMS, preferred_element_type=F32))
            gate = jnp.where(eligible, gate, -jnp.inf)
            beaten = jnp.zeros((nblk, blk), F32)
            for j in range(nblk):
                row = gate[j:j + 1, :]
                wins = (row > gate) | ((row == gate) & (bidx > j))
                beaten = beaten + jnp.where(wins, 1.0, 0.0)
            sel = jnp.where((beaten < MOBA_TOPK) & eligible, 1.0, 0.0)
            sel_ref[h, 0] = jnp.zeros((1, blk), F32)
            for j in range(nblk):
                sel_ref[h, j + 1] = sel[j:j + 1, :]

    m_ref[...] = jnp.full(m_ref.shape, NEG, F32)
    l_ref[...] = jnp.zeros_like(l_ref)
    acc_ref[...] = jnp.zeros_like(acc_ref)

    def scores(slot, kblock):
        off = pl.multiple_of(kblock * blk, blk)
        for pr in range(npairs):
            kb = k_ref[0, pl.ds(off, blk), pr * LANES:(pr + 1) * LANES]
            for hh in range(2):
                h = 2 * pr + hh
                s_ref[slot, h] = lax.dot_general(kb, qh_ref[h], NT_DIMS, preferred_element_type=F32)

    def accumulate(slot, kblock, sel_row):
        off = pl.multiple_of(kblock * blk, blk)
        for pr in range(npairs):
            vb = v_ref[0, pl.ds(off, blk), pr * LANES:(pr + 1) * LANES]
            for hh in range(2):
                h = 2 * pr + hh
                st = s_ref[slot, h]
                m = m_ref[h]
                if sel_row is None:
                    causal = (lax.broadcasted_iota(jnp.int32, (blk, blk), 0)
                              <= lax.broadcasted_iota(jnp.int32, (blk, blk), 1))
                    st = jnp.where(causal, st, NEG)
                    m_new = jnp.maximum(m, jnp.max(st, axis=0, keepdims=True))
                    p = jnp.exp(st - m_new)
                else:
                    picked = sel_ref[h, sel_row] > 0.0
                    m_new = jnp.maximum(m, jnp.where(picked, jnp.max(st, axis=0, keepdims=True), NEG))
                    p = jnp.exp(st - jnp.where(picked, m_new, -NEG))
                alpha = jnp.exp(m - m_new)
                m_ref[h] = m_new
                l_ref[h] = alpha * l_ref[h] + jnp.sum(p, axis=0, keepdims=True)
                pv = lax.dot_general(vb, p.astype(BF16), TN_DIMS, preferred_element_type=F32)
                acc_ref[h] = alpha * acc_ref[h] + pv[hh * HEAD_DIM:(hh + 1) * HEAD_DIM]

    pad = (i + 1) & 1
    n_pairs = (i + 1 + pad) // 2
    scores(0, 0)

    def pair(t, _):
        a = 2 * t - pad
        scores(1, a + 1)
        accumulate(0, jnp.maximum(a, 0), a + 1)
        scores(0, a + 2)
        accumulate(1, a + 1, a + 2)
        return 0

    lax.fori_loop(0, n_pairs - 1, pair, 0)
    scores(1, i)
    accumulate(0, jnp.maximum(i - 1, 0), i)
    accumulate(1, i, None)

    for pr in range(npairs):
        out_t = jnp.concatenate([acc_ref[2 * pr] / l_ref[2 * pr], acc_ref[2 * pr + 1] / l_ref[2 * pr + 1]], axis=0)
        o_ref[0, :, pr * LANES:(pr + 1) * LANES] = out_t.T.astype(BF16)


def _moba(q, k, v, kmean, *, npairs):
    B, S, W = q.shape
    nblk = S // MOBA_BLOCK
    wstep = npairs * LANES
    kern = functools.partial(_moba_kernel, nblk=nblk, npairs=npairs)
    return pl.pallas_call(
        kern,
        grid=(B, W // wstep, nblk),
        in_specs=[
            pl.BlockSpec((1, MOBA_BLOCK, wstep), lambda b, g, i: (b, i, g)),
            pl.BlockSpec((1, S, wstep), lambda b, g, i: (b, 0, g)),
            pl.BlockSpec((1, S, wstep), lambda b, g, i: (b, 0, g)),
            pl.BlockSpec((1, nblk, wstep), lambda b, g, i: (b, 0, g)),
        ],
        out_specs=pl.BlockSpec((1, MOBA_BLOCK, wstep), lambda b, g, i: (b, i, g)),
        out_shape=jax.ShapeDtypeStruct((B, S, W), BF16),
        scratch_shapes=[pltpu.VMEM((2 * npairs, MOBA_BLOCK, LANES), BF16),
                        pltpu.VMEM((2 * npairs, nblk + 1, 1, MOBA_BLOCK), F32),
                        pltpu.VMEM((2, 2 * npairs, MOBA_BLOCK, MOBA_BLOCK), F32),
                        pltpu.VMEM((2 * npairs, 1, MOBA_BLOCK), F32),
                        pltpu.VMEM((2 * npairs, 1, MOBA_BLOCK), F32),
                        pltpu.VMEM((2 * npairs, HEAD_DIM, MOBA_BLOCK), F32)],
        compiler_params=pltpu.CompilerParams(
            dimension_semantics=("arbitrary", "arbitrary", "arbitrary"), vmem_limit_bytes=VMEM_LIMIT),
        name="moba",
    )(q, k, v, kmean)


def _mem_kv_kernel(mem_ref, g_ref, w_ref, kg_ref, k_ref, v_ref, *, d_model):
    mb = _rms(mem_ref[0], g_ref[...]).astype(BF16)
    hd = d_model // XATTN_HEADS
    for h in range(XATTN_HEADS):
        lanes = slice(h * hd, (h + 1) * hd)
        kh = jnp.dot(mb, w_ref[:, lanes], preferred_element_type=F32)
        k_ref[0, :, lanes] = _rms(kh, kg_ref[...]).astype(BF16)
    v_ref[0] = jnp.dot(mb, w_ref[:, d_model:2 * d_model], preferred_element_type=F32).astype(BF16)


def _mem_kv(mem, g, w_xkv, kg):
    B, M, D = mem.shape
    const2 = lambda b: (0, 0)
    blk3 = lambda b: (b, 0, 0)
    return pl.pallas_call(
        functools.partial(_mem_kv_kernel, d_model=D),
        grid=(B,),
        in_specs=[pl.BlockSpec((1, M, D), blk3), pl.BlockSpec((1, D), const2),
                  pl.BlockSpec(w_xkv.shape, const2), pl.BlockSpec((1, D // XATTN_HEADS), const2)],
        out_specs=[pl.BlockSpec((1, M, D), blk3), pl.BlockSpec((1, M, D), blk3)],
        out_shape=[jax.ShapeDtypeStruct((B, M, D), BF16)] * 2,
        compiler_params=pltpu.CompilerParams(
            dimension_semantics=("arbitrary",), vmem_limit_bytes=VMEM_LIMIT),
        name="mem_kv",
    )(mem, g, w_xkv, kg)


def _mix_xattn_kernel(x_ref, pool_ref, attn_ref, wo_ref, g_ref, wq_ref, qg_ref, k_ref, v_ref, wxo_ref,
                      o_ref, oh_ref, *, pool_width, d_model):
    x1 = (x_ref[0]
          + jnp.dot(pool_ref[0], wo_ref[0:pool_width, :], preferred_element_type=F32)
          + jnp.dot(attn_ref[0], wo_ref[pool_width:, :], preferred_element_type=F32))
    hb = _rms(x1, g_ref[...]).astype(BF16)
    hd = d_model // XATTN_HEADS
    for h in range(XATTN_HEADS):
        lanes = slice(h * hd, (h + 1) * hd)
        qh = jnp.dot(hb, wq_ref[:, lanes], preferred_element_type=F32)
        qh = (_rms(qh, qg_ref[...]) * (hd ** -0.5)).astype(BF16)
        s = lax.dot_general(qh, k_ref[0, :, lanes], NT_DIMS, preferred_element_type=F32)
        p = jnp.exp(s - jnp.max(s, axis=-1, keepdims=True))
        l = jnp.sum(p, axis=-1, keepdims=True)
        oh = jnp.dot(p.astype(BF16), v_ref[0, :, lanes], preferred_element_type=F32) / l
        oh_ref[:, lanes] = oh.astype(BF16)
    o_ref[0] = x1 + jnp.dot(oh_ref[...], wxo_ref[...], preferred_element_type=F32)


def _mix_xattn(x, pool, attn, w_out, g, w_xq, qg, kx, vx, w_xo, *, tm):
    B, S, D = x.shape
    M = kx.shape[1]
    pool_width = pool.shape[2]
    const2 = lambda b, s: (0, 0)
    tile3 = lambda b, s: (b, s, 0)
    mem3 = lambda b, s: (b, 0, 0)
    return pl.pallas_call(
        functools.partial(_mix_xattn_kernel, pool_width=pool_width, d_model=D),
        grid=(B, S // tm),
        in_specs=[
            pl.BlockSpec((1, tm, D), tile3),
            pl.BlockSpec((1, tm, pool_width), tile3),
            pl.BlockSpec((1, tm, attn.shape[2]), tile3),
            pl.BlockSpec(w_out.shape, const2),
            pl.BlockSpec((1, D), const2),
            pl.BlockSpec(w_xq.shape, const2),
            pl.BlockSpec((1, D // XATTN_HEADS), const2),
            pl.BlockSpec((1, M, D), mem3),
            pl.BlockSpec((1, M, D), mem3),
            pl.BlockSpec(w_xo.shape, const2),
        ],
        out_specs=pl.BlockSpec((1, tm, D), tile3),
        out_shape=jax.ShapeDtypeStruct((B, S, D), F32),
        scratch_shapes=[pltpu.VMEM((tm, D), BF16)],
        compiler_params=pltpu.CompilerParams(
            dimension_semantics=("arbitrary", "arbitrary"), vmem_limit_bytes=VMEM_LIMIT),
        name="mix_xattn",
    )(x, pool, attn, w_out, g, w_xq, qg, kx, vx, w_xo)


def _conv_ffn_kernel(x_ref, g_ref, wu_ref, cw_ref, cb_ref, wd_ref, o_ref,
                     hb_ref, halo_ref, buf_ref, act_ref, *, tm, n_chunks):
    s = pl.program_id(1)

    @pl.when(s == 0)
    def _():
        halo_ref[...] = jnp.zeros_like(halo_ref)

    x = x_ref[0]
    hb_ref[...] = _rms(x, g_ref[...]).astype(BF16)

    def conv(c, slot):
        up = jnp.dot(hb_ref[...], wu_ref[c], preferred_element_type=F32)
        buf_ref[slot, 0:CONV_HALO, :] = halo_ref[c]
        buf_ref[slot, CONV_HALO:CONV_HALO + tm, :] = up
        halo_ref[c] = up[tm - CONV_HALO:, :]
        cw = cw_ref[c]
        y = up * cw[CONV_WIDTH - 1:CONV_WIDTH, :] + cb_ref[c]
        for d in range(1, CONV_WIDTH):
            y = y + buf_ref[slot, CONV_HALO - d:CONV_HALO - d + tm, :] * cw[CONV_WIDTH - 1 - d:CONV_WIDTH - d, :]
        return y

    for c in range(n_chunks):
        gate = conv(c, 2 * (c % 2))
        val = conv(c + n_chunks, 2 * (c % 2) + 1)
        act_ref[:, c * FF_CHUNK:(c + 1) * FF_CHUNK] = (gate * (1.0 / (1.0 + jnp.exp(-gate))) * val).astype(BF16)
    o_ref[0] = x + jnp.dot(act_ref[...], wd_ref[...], preferred_element_type=F32)


def _conv_ffn(x, g, w_up_c, conv_w_c, conv_b_c, w_down, *, tm):
    B, S, D = x.shape
    d_ff = w_down.shape[0]
    n_chunks = d_ff // FF_CHUNK
    const2 = lambda b, s: (0, 0)
    const3 = lambda b, s: (0, 0, 0)
    tile3 = lambda b, s: (b, s, 0)
    return pl.pallas_call(
        functools.partial(_conv_ffn_kernel, tm=tm, n_chunks=n_chunks),
        grid=(B, S // tm),
        in_specs=[
            pl.BlockSpec((1, tm, D), tile3),
            pl.BlockSpec((1, D), const2),
            pl.BlockSpec(w_up_c.shape, const3),
            pl.BlockSpec(conv_w_c.shape, const3),
            pl.BlockSpec(conv_b_c.shape, const3),
            pl.BlockSpec(w_down.shape, const2),
        ],
        out_specs=pl.BlockSpec((1, tm, D), tile3),
        out_shape=jax.ShapeDtypeStruct((B, S, D), F32),
        scratch_shapes=[
            pltpu.VMEM((tm, D), BF16),
            pltpu.VMEM((2 * n_chunks, CONV_HALO, FF_CHUNK), F32),
            pltpu.VMEM((4, CONV_HALO + tm, FF_CHUNK), F32),
            pltpu.VMEM((tm, d_ff), BF16),
        ],
        compiler_params=pltpu.CompilerParams(
            dimension_semantics=("arbitrary", "arbitrary"), vmem_limit_bytes=VMEM_LIMIT),
        name="conv_ffn",
    )(x, g, w_up_c, conv_w_c, conv_b_c, w_down)


def _rope_tables(S):
    half = HEAD_DIM // 2
    inv_freq = ROPE_THETA ** (-jnp.arange(half, dtype=F32) / half)
    ang = jnp.arange(S).astype(F32)[:, None] * inv_freq[None, :]
    cos, sin = jnp.cos(ang), jnp.sin(ang)
    return (jnp.concatenate([cos, cos, cos, cos], axis=-1),
            jnp.concatenate([-sin, -sin, sin, sin], axis=-1))


def _slab_gain(g):
    lo, hi = g[:HEAD_DIM // 2], g[HEAD_DIM // 2:]
    return jnp.concatenate([lo, lo, hi, hi]).reshape(1, LANES)


def _slab_columns(width):
    half = HEAD_DIM // 2
    idx = jnp.arange(width).reshape(width // LANES, 2, 2, half)
    return idx.transpose(0, 2, 1, 3).reshape(width)


def kernel(x, mem, norm_mix_g, w_in, pool_w, pool_scale, q_norm_g, k_norm_g, w_out, norm_xattn_g, norm_mem_g, w_xq, w_xkv, xq_norm_g, xk_norm_g, w_xo, norm_ffn_g, w_up, conv_w, conv_b, w_down):
    B, S, D = x.shape
    depth = w_in.shape[0]
    d_ff = w_down.shape[1]
    assert S % 512 == 0 and d_ff % FF_CHUNK == 0
    n_chunks = d_ff // FF_CHUNK
    cos, sin = _rope_tables(S)
    row = lambda a: a.reshape(1, -1)
    pool_width = pool_w.shape[1] * pool_w.shape[2]
    attn_width = (w_in.shape[2] - pool_width) // 3
    perm = _slab_columns(attn_width)
    in_cols = jnp.concatenate([jnp.arange(pool_width), pool_width + perm, pool_width + attn_width + perm,
                               pool_width + 2 * attn_width + jnp.arange(attn_width)])
    for l in range(depth):
        pool, q, k, v, kmean = _in_proj(
            x, row(norm_mix_g[l]), w_in[l][:, in_cols].astype(BF16), pool_w[l].astype(BF16), row(pool_scale[l]),
            _slab_gain(q_norm_g[l]), _slab_gain(k_norm_g[l]), cos, sin, tm=512)
        attn = _moba(q, k, v, kmean.reshape(B, S // MOBA_BLOCK, -1), npairs=2)
        kx, vx = _mem_kv(mem, row(norm_mem_g[l]), w_xkv[l].astype(BF16), row(xk_norm_g[l]))
        x = _mix_xattn(x, pool, attn, w_out[l].astype(BF16), row(norm_xattn_g[l]), w_xq[l].astype(BF16),
                       row(xq_norm_g[l]), kx, vx, w_xo[l].astype(BF16), tm=512)
        w_up_c = w_up[l].astype(BF16).reshape(D, 2 * n_chunks, FF_CHUNK).transpose(1, 0, 2)
        conv_w_c = conv_w[l].reshape(CONV_WIDTH, 2 * n_chunks, FF_CHUNK).transpose(1, 0, 2)
        conv_b_c = conv_b[l].reshape(2 * n_chunks, 1, FF_CHUNK)
        x = _conv_ffn(x, row(norm_ffn_g[l]), w_up_c, conv_w_c, conv_b_c, w_down[l].astype(BF16), tm=512)
    return x
```

```python
import functools

import jax
import jax.numpy as jnp
from jax import lax
from jax.experimental import pallas as pl
from jax.experimental.pallas import tpu as pltpu

F32 = jnp.float32
BF16 = jnp.bfloat16

EPS = 1e-6
LANES = 128
SUBLANES = 8
MXU_COLS = 256
HEAD_DIM = 64
POOL_WINDOWS = (2, 4, 8, 16)
POOL_HALO = SUBLANES * len(POOL_WINDOWS)
MOBA_BLOCK = 256
MOBA_TOPK = 3
ROPE_THETA = 10000.0
XATTN_HEADS = 4
CONV_WIDTH = 3
CONV_HALO = 8
FF_CHUNK = 256
NEG = -1e30
LOG2E = 1.4426950408889634
VMEM_LIMIT = 56 * 1024 * 1024

NT_DIMS = (((1,), (1,)), ((), ()))
TN_DIMS = (((0,), (0,)), ((), ()))


def _rms(x, g):
    return x * lax.rsqrt(jnp.mean(x * x, axis=-1, keepdims=True) + EPS) * g


def _first_head_lanes():
    lane = lax.broadcasted_iota(jnp.int32, (1, LANES), 1)
    return (lane & (HEAD_DIM // 2)) == 0


def _head_norm_rope(y, g, cos, sin_signed):
    first = _first_head_lanes()
    sq = y * y
    s0 = jnp.sum(jnp.where(first, sq, 0.0), axis=-1, keepdims=True)
    s1 = jnp.sum(jnp.where(first, 0.0, sq), axis=-1, keepdims=True)
    r = lax.rsqrt(jnp.where(first, s0, s1) * (1.0 / HEAD_DIM) + EPS)
    yn = y * r * g
    return yn * cos + pltpu.roll(yn, LANES // 2, axis=1) * sin_signed


def _in_proj_kernel(x_ref, g_ref, w_ref, pw_ref, ps_ref, qg_ref, kg_ref, cos_ref, sin_ref,
                    pool_ref, q_ref, k_ref, v_ref, kmean_ref, ubuf_ref, lva_ref, lvb_ref, y_ref,
                    *, tm, pool_width, attn_width):
    s = pl.program_id(1)

    @pl.when(s == 0)
    def _():
        ubuf_ref[0:POOL_HALO, :] = jnp.zeros((POOL_HALO, pool_width), F32)

    hb = _rms(x_ref[0], g_ref[...]).astype(BF16)
    q0, k0, v0 = pool_width, pool_width + attn_width, pool_width + 2 * attn_width

    y_ref[...] = jnp.dot(hb, w_ref[:, q0:v0], preferred_element_type=F32)
    cos = cos_ref[...]
    sin = sin_ref[...]
    qg = qg_ref[...] * (HEAD_DIM ** -0.5 * LOG2E)
    for c in range(attn_width // LANES):
        lanes = slice(c * LANES, (c + 1) * LANES)
        q_ref[0, :, lanes] = _head_norm_rope(y_ref[:, lanes], qg, cos, sin).astype(BF16)
        kr = _head_norm_rope(y_ref[:, attn_width + c * LANES:attn_width + (c + 1) * LANES], kg_ref[...], cos, sin)
        k_ref[0, :, lanes] = kr.astype(BF16)
        for r in range(tm // MOBA_BLOCK):
            km = jnp.sum(kr[r * MOBA_BLOCK:(r + 1) * MOBA_BLOCK], axis=0, keepdims=True) * (1.0 / MOBA_BLOCK)
            kmean_ref[0, r, :, lanes] = km

    u = jnp.dot(hb, w_ref[:, 0:pool_width], preferred_element_type=F32)
    end = POOL_HALO + tm
    ubuf_ref[POOL_HALO:end, :] = u
    gd = pool_width // len(POOL_WINDOWS)
    src = ubuf_ref
    for k in range(1, len(POOL_WINDOWS) + 1):
        dst = lva_ref if k % 2 else lvb_ref
        assert POOL_WINDOWS[k - 1] == 2 ** k
        shift, r0, l0 = 2 ** (k - 1), SUBLANES * k, (k - 1) * gd
        dst[r0:end, l0:] = src[r0:end, l0:] + src[r0 - shift:end - shift, l0:]
        src = dst
    t = s * tm + lax.broadcasted_iota(jnp.int32, (tm, 1), 0)
    for g, w in enumerate(POOL_WINDOWS):
        lanes = slice(g * gd, (g + 1) * gd)
        win = (lva_ref if (g + 1) % 2 else lvb_ref)[POOL_HALO:end, lanes]
        cnt = jnp.minimum(t + 1, w).astype(F32)
        pooled = win / cnt - u[:, lanes]
        mixed = jnp.dot(pooled.astype(BF16), pw_ref[g], preferred_element_type=F32)
        pool_ref[0, :, lanes] = (mixed * ps_ref[:, lanes]).astype(BF16)
    ubuf_ref[0:POOL_HALO, :] = ubuf_ref[tm:end, :]

    v = jnp.dot(hb, w_ref[:, v0:v0 + attn_width], preferred_element_type=F32)
    lane = lax.broadcasted_iota(jnp.int32, (1, LANES), 1)
    for pr in range(attn_width // LANES):
        pair = v[:, pr * LANES:(pr + 1) * LANES]
        for hh in range(2):
            keep = (lane < HEAD_DIM) if hh == 0 else (lane >= HEAD_DIM)
            ones = jnp.where(lane == _ones_lane(hh), 1.0, 0.0)
            h = 2 * pr + hh
            v_ref[0, :, h * LANES:(h + 1) * LANES] = jnp.where(keep, pair, ones).astype(BF16)


def _ones_lane(hh):
    return HEAD_DIM if hh == 0 else 0


def _in_proj(x, g, w_in, pool_w, pool_scale, qg, kg, cos, sin, *, tm):
    B, S, D = x.shape
    n_groups, gd, _ = pool_w.shape
    pool_width = n_groups * gd
    attn_width = (w_in.shape[1] - pool_width) // 3
    nblk = S // MOBA_BLOCK
    kern = functools.partial(_in_proj_kernel, tm=tm, pool_width=pool_width, attn_width=attn_width)
    const2 = lambda b, s: (0, 0)
    tile3 = lambda b, s: (b, s, 0)
    return pl.pallas_call(
        kern,
        grid=(B, S // tm),
        in_specs=[
            pl.BlockSpec((1, tm, D), tile3),
            pl.BlockSpec((1, D), const2),
            pl.BlockSpec(w_in.shape, const2),
            pl.BlockSpec(pool_w.shape, lambda b, s: (0, 0, 0)),
            pl.BlockSpec((1, pool_width), const2),
            pl.BlockSpec((1, LANES), const2),
            pl.BlockSpec((1, LANES), const2),
            pl.BlockSpec((tm, LANES), lambda b, s: (s, 0)),
            pl.BlockSpec((tm, LANES), lambda b, s: (s, 0)),
        ],
        out_specs=[
            pl.BlockSpec((1, tm, pool_width), tile3),
            pl.BlockSpec((1, tm, attn_width), tile3),
            pl.BlockSpec((1, tm, attn_width), tile3),
            pl.BlockSpec((1, tm, 2 * attn_width), tile3),
            pl.BlockSpec((1, tm // MOBA_BLOCK, 1, attn_width), lambda b, s: (b, s, 0, 0)),
        ],
        out_shape=[
            jax.ShapeDtypeStruct((B, S, pool_width), BF16),
            jax.ShapeDtypeStruct((B, S, attn_width), BF16),
            jax.ShapeDtypeStruct((B, S, attn_width), BF16),
            jax.ShapeDtypeStruct((B, S, 2 * attn_width), BF16),
            jax.ShapeDtypeStruct((B, nblk, 1, attn_width), F32),
        ],
        scratch_shapes=[pltpu.VMEM((POOL_HALO + tm, pool_width), F32)] * 3 + [pltpu.VMEM((tm, 2 * attn_width), F32)],
        compiler_params=pltpu.CompilerParams(
            dimension_semantics=("arbitrary", "arbitrary"), vmem_limit_bytes=VMEM_LIMIT),
        name="in_proj",
    )(x, g, w_in, pool_w, pool_scale, qg, kg, cos, sin)


def _moba_kernel(q_ref, k_ref, v_ref, kmean_ref, o_ref, qh_ref, sel_ref, s_ref, m_ref, acc_ref, *, nblk, npairs):
    i = pl.program_id(2)
    blk = MOBA_BLOCK
    first = _first_head_lanes()
    bidx = lax.broadcasted_iota(jnp.int32, (nblk, 1), 0)
    eligible = bidx < i

    for pr in range(npairs):
        lanes = slice(pr * LANES, (pr + 1) * LANES)
        qs = q_ref[0, :, lanes]
        zero = jnp.zeros_like(qs)
        km = kmean_ref[0, :, lanes]
        km_hi = km.astype(BF16)
        km_lo = (km - km_hi.astype(F32)).astype(BF16)
        for hh in range(2):
            h = 2 * pr + hh
            qm = jnp.where(first, qs, zero) if hh == 0 else jnp.where(first, zero, qs)
            qh_ref[h] = qm
            gate = (lax.dot_general(km_hi, qm, NT_DIMS, preferred_element_type=F32)
                    + lax.dot_general(km_lo, qm, NT_DIMS, preferred_element_type=F32))
            gate = jnp.where(eligible, gate, -jnp.inf)
            beaten = jnp.zeros((nblk, blk), F32)
            for j in range(nblk):
                row = gate[j:j + 1, :]
                wins = (row > gate) | ((row == gate) & (bidx > j))
                beaten = beaten + jnp.where(wins, 1.0, 0.0)
            sel = jnp.where((beaten < MOBA_TOPK) & eligible, 1.0, 0.0)
            sel_ref[h, 0] = jnp.zeros((1, blk), F32)
            for j in range(nblk):
                sel_ref[h, j + 1] = sel[j:j + 1, :]

    m_ref[...] = jnp.full(m_ref.shape, NEG, F32)
    acc_ref[...] = jnp.zeros_like(acc_ref)

    def scores(slot, kblock):
        off = pl.multiple_of(kblock * blk, blk)
        for pr in range(npairs):
            kb = k_ref[0, pl.ds(off, blk), pr * LANES:(pr + 1) * LANES]
            for hh in range(2):
                h = 2 * pr + hh
                s_ref[slot, h] = lax.dot_general(kb, qh_ref[h], NT_DIMS, preferred_element_type=F32)

    def accumulate(slot, kblock, sel_row):
        off = pl.multiple_of(kblock * blk, blk)
        for pr in range(npairs):
            for hh in range(2):
                h = 2 * pr + hh
                vb = v_ref[0, pl.ds(off, blk), h * LANES:(h + 1) * LANES]
                st = s_ref[slot, h]
                m = m_ref[h]
                if sel_row is None:
                    causal = (lax.broadcasted_iota(jnp.int32, (blk, blk), 0)
                              <= lax.broadcasted_iota(jnp.int32, (blk, blk), 1))
                    st = jnp.where(causal, st, NEG)
                    m_new = jnp.maximum(m, jnp.max(st, axis=0, keepdims=True))
                    p = jnp.exp2(st - m_new)
                else:
                    picked = sel_ref[h, sel_row] > 0.0
                    m_new = jnp.maximum(m, jnp.where(picked, jnp.max(st, axis=0, keepdims=True), NEG))
                    p = jnp.exp2(st - jnp.where(picked, m_new, -NEG))
                alpha = jnp.exp2(m - m_new)
                m_ref[h] = m_new
                pv = lax.dot_general(vb, p.astype(BF16), TN_DIMS, preferred_element_type=F32)
                ones_row = _ones_lane(hh)
                acc_ref[h, 0:HEAD_DIM] = alpha * acc_ref[h, 0:HEAD_DIM] + pv[hh * HEAD_DIM:(hh + 1) * HEAD_DIM]
                acc_ref[h, HEAD_DIM:] = alpha * acc_ref[h, HEAD_DIM:] + pv[ones_row:ones_row + SUBLANES]

    pad = (i + 1) & 1
    n_pairs = (i + 1 + pad) // 2
    scores(0, 0)

    def pair(t, _):
        a = 2 * t - pad
        scores(1, a + 1)
        accumulate(0, jnp.maximum(a, 0), a + 1)
        scores(0, a + 2)
        accumulate(1, a + 1, a + 2)
        return 0

    lax.fori_loop(0, n_pairs - 1, pair, 0)
    scores(1, i)
    accumulate(0, jnp.maximum(i - 1, 0), i)
    accumulate(1, i, None)

    for pr in range(npairs):
        halves = [acc_ref[h, 0:HEAD_DIM] / acc_ref[h, HEAD_DIM:HEAD_DIM + 1] for h in (2 * pr, 2 * pr + 1)]
        o_ref[0, :, pr * LANES:(pr + 1) * LANES] = jnp.concatenate(halves, axis=0).T.astype(BF16)


def _moba(q, k, v, kmean, *, npairs):
    B, S, W = q.shape
    nblk = S // MOBA_BLOCK
    wstep = npairs * LANES
    kern = functools.partial(_moba_kernel, nblk=nblk, npairs=npairs)
    return pl.pallas_call(
        kern,
        grid=(B, W // wstep, nblk),
        in_specs=[
            pl.BlockSpec((1, MOBA_BLOCK, wstep), lambda b, g, i: (b, i, g)),
            pl.BlockSpec((1, S, wstep), lambda b, g, i: (b, 0, g)),
            pl.BlockSpec((1, S, 2 * wstep), lambda b, g, i: (b, 0, g)),
            pl.BlockSpec((1, nblk, wstep), lambda b, g, i: (b, 0, g)),
        ],
        out_specs=pl.BlockSpec((1, MOBA_BLOCK, wstep), lambda b, g, i: (b, i, g)),
        out_shape=jax.ShapeDtypeStruct((B, S, W), BF16),
        scratch_shapes=[pltpu.VMEM((2 * npairs, MOBA_BLOCK, LANES), BF16),
                        pltpu.VMEM((2 * npairs, nblk + 1, 1, MOBA_BLOCK), F32),
                        pltpu.VMEM((2, 2 * npairs, MOBA_BLOCK, MOBA_BLOCK), F32),
                        pltpu.VMEM((2 * npairs, 1, MOBA_BLOCK), F32),
                        pltpu.VMEM((2 * npairs, HEAD_DIM + SUBLANES, MOBA_BLOCK), F32)],
        compiler_params=pltpu.CompilerParams(
            dimension_semantics=("arbitrary", "arbitrary", "arbitrary"), vmem_limit_bytes=VMEM_LIMIT),
        name="moba",
    )(q, k, v, kmean)


def _mem_kv_kernel(mem_ref, g_ref, w_ref, kg_ref, k_ref, v_ref, *, d_model):
    mb = _rms(mem_ref[0], g_ref[...]).astype(BF16)
    hd = d_model // XATTN_HEADS
    for h in range(XATTN_HEADS):
        lanes = slice(h * hd, (h + 1) * hd)
        kh = jnp.dot(mb, w_ref[:, lanes], preferred_element_type=F32)
        k_ref[0, :, lanes] = _rms(kh, kg_ref[...]).astype(BF16)
    v_ref[0] = jnp.dot(mb, w_ref[:, d_model:2 * d_model], preferred_element_type=F32).astype(BF16)


def _mem_kv(mem, g, w_xkv, kg):
    B, M, D = mem.shape
    const2 = lambda b: (0, 0)
    blk3 = lambda b: (b, 0, 0)
    return pl.pallas_call(
        functools.partial(_mem_kv_kernel, d_model=D),
        grid=(B,),
        in_specs=[pl.BlockSpec((1, M, D), blk3), pl.BlockSpec((1, D), const2),
                  pl.BlockSpec(w_xkv.shape, const2), pl.BlockSpec((1, D // XATTN_HEADS), const2)],
        out_specs=[pl.BlockSpec((1, M, D), blk3), pl.BlockSpec((1, M, D), blk3)],
        out_shape=[jax.ShapeDtypeStruct((B, M, D), BF16)] * 2,
        compiler_params=pltpu.CompilerParams(
            dimension_semantics=("arbitrary",), vmem_limit_bytes=VMEM_LIMIT),
        name="mem_kv",
    )(mem, g, w_xkv, kg)


def _mix_xattn_kernel(x_ref, pool_ref, attn_ref, wo_ref, g_ref, wq_ref, qg_ref, k_ref, v_ref, wxo_ref,
                      o_ref, q_ref, qn_ref, s_ref, p_ref, oh_ref, *, pool_width, d_model):
    x1 = (x_ref[0]
          + jnp.dot(pool_ref[0], wo_ref[0:pool_width, :], preferred_element_type=F32)
          + jnp.dot(attn_ref[0], wo_ref[pool_width:, :], preferred_element_type=F32))
    hb = _rms(x1, g_ref[...]).astype(BF16)
    hd = d_model // XATTN_HEADS
    heads = [slice(h * hd, (h + 1) * hd) for h in range(XATTN_HEADS)]
    q_ref[...] = jnp.dot(hb, wq_ref[...], preferred_element_type=F32)
    for lanes in heads:
        qn_ref[:, lanes] = (_rms(q_ref[:, lanes], qg_ref[...]) * (hd ** -0.5)).astype(BF16)
    for h, lanes in enumerate(heads):
        s_ref[h] = lax.dot_general(qn_ref[:, lanes], k_ref[0, :, lanes], NT_DIMS,
                                   preferred_element_type=F32)
    for h in range(XATTN_HEADS):
        s = s_ref[h]
        p = jnp.exp(s - jnp.max(s, axis=-1, keepdims=True))
        p_ref[h] = (p / jnp.sum(p, axis=-1, keepdims=True)).astype(BF16)
    for h, lanes in enumerate(heads):
        oh_ref[:, lanes] = jnp.dot(p_ref[h], v_ref[0, :, lanes], preferred_element_type=F32).astype(BF16)
    o_ref[0] = x1 + jnp.dot(oh_ref[...], wxo_ref[...], preferred_element_type=F32)


def _mix_xattn(x, pool, attn, w_out, g, w_xq, qg, kx, vx, w_xo, *, tm):
    B, S, D = x.shape
    M = kx.shape[1]
    pool_width = pool.shape[2]
    const2 = lambda b, s: (0, 0)
    tile3 = lambda b, s: (b, s, 0)
    mem3 = lambda b, s: (b, 0, 0)
    return pl.pallas_call(
        functools.partial(_mix_xattn_kernel, pool_width=pool_width, d_model=D),
        grid=(B, S // tm),
        in_specs=[
            pl.BlockSpec((1, tm, D), tile3),
            pl.BlockSpec((1, tm, pool_width), tile3),
            pl.BlockSpec((1, tm, attn.shape[2]), tile3),
            pl.BlockSpec(w_out.shape, const2),
            pl.BlockSpec((1, D), const2),
            pl.BlockSpec(w_xq.shape, const2),
            pl.BlockSpec((1, D // XATTN_HEADS), const2),
            pl.BlockSpec((1, M, D), mem3),
            pl.BlockSpec((1, M, D), mem3),
            pl.BlockSpec(w_xo.shape, const2),
        ],
        out_specs=pl.BlockSpec((1, tm, D), tile3),
        out_shape=jax.ShapeDtypeStruct((B, S, D), F32),
        scratch_shapes=[pltpu.VMEM((tm, D), F32), pltpu.VMEM((tm, D), BF16),
                        pltpu.VMEM((XATTN_HEADS, tm, M), F32), pltpu.VMEM((XATTN_HEADS, tm, M), BF16),
                        pltpu.VMEM((tm, D), BF16)],
        compiler_params=pltpu.CompilerParams(
            dimension_semantics=("arbitrary", "arbitrary"), vmem_limit_bytes=VMEM_LIMIT),
        name="mix_xattn",
    )(x, pool, attn, w_out, g, w_xq, qg, kx, vx, w_xo)


def _conv_ffn_kernel(x_ref, g_ref, wu_ref, cw_ref, cb_ref, wd_ref, o_ref,
                     hb_ref, halo_ref, buf_ref, act_ref, *, tm, n_chunks):
    s = pl.program_id(1)

    @pl.when(s == 0)
    def _():
        halo_ref[...] = jnp.zeros_like(halo_ref)

    x = x_ref[0]
    hb_ref[...] = _rms(x, g_ref[...]).astype(BF16)

    def conv(c, slot):
        up = jnp.dot(hb_ref[...], wu_ref[c], preferred_element_type=F32)
        buf_ref[slot, 0:CONV_HALO, :] = halo_ref[c]
        buf_ref[slot, CONV_HALO:CONV_HALO + tm, :] = up
        halo_ref[c] = up[tm - CONV_HALO:, :]
        cw = cw_ref[c]
        y = up * cw[CONV_WIDTH - 1:CONV_WIDTH, :] + cb_ref[c]
        for d in range(1, CONV_WIDTH):
            y = y + buf_ref[slot, CONV_HALO - d:CONV_HALO - d + tm, :] * cw[CONV_WIDTH - 1 - d:CONV_WIDTH - d, :]
        return y

    for c in range(n_chunks):
        gate = conv(c, 2 * (c % 2))
        val = conv(c + n_chunks, 2 * (c % 2) + 1)
        act_ref[:, c * FF_CHUNK:(c + 1) * FF_CHUNK] = (gate * (1.0 / (1.0 + jnp.exp(-gate))) * val).astype(BF16)
    o_ref[0] = x + jnp.dot(act_ref[...], wd_ref[...], preferred_element_type=F32)


def _conv_ffn(x, g, w_up_c, conv_w_c, conv_b_c, w_down, *, tm):
    B, S, D = x.shape
    d_ff = w_down.shape[0]
    n_chunks = d_ff // FF_CHUNK
    const2 = lambda b, s: (0, 0)
    const3 = lambda b, s: (0, 0, 0)
    tile3 = lambda b, s: (b, s, 0)
    return pl.pallas_call(
        functools.partial(_conv_ffn_kernel, tm=tm, n_chunks=n_chunks),
        grid=(B, S // tm),
        in_specs=[
            pl.BlockSpec((1, tm, D), tile3),
            pl.BlockSpec((1, D), const2),
            pl.BlockSpec(w_up_c.shape, const3),
            pl.BlockSpec(conv_w_c.shape, const3),
            pl.BlockSpec(conv_b_c.shape, const3),
            pl.BlockSpec(w_down.shape, const2),
        ],
        out_specs=pl.BlockSpec((1, tm, D), tile3),
        out_shape=jax.ShapeDtypeStruct((B, S, D), F32),
        scratch_shapes=[
            pltpu.VMEM((tm, D), BF16),
            pltpu.VMEM((2 * n_chunks, CONV_HALO, FF_CHUNK), F32),
            pltpu.VMEM((4, CONV_HALO + tm, FF_CHUNK), F32),
            pltpu.VMEM((tm, d_ff), BF16),
        ],
        compiler_params=pltpu.CompilerParams(
            dimension_semantics=("arbitrary", "arbitrary"), vmem_limit_bytes=VMEM_LIMIT),
        name="conv_ffn",
    )(x, g, w_up_c, conv_w_c, conv_b_c, w_down)


def _rope_tables(S):
    half = HEAD_DIM // 2
    inv_freq = ROPE_THETA ** (-jnp.arange(half, dtype=F32) / half)
    ang = jnp.arange(S).astype(F32)[:, None] * inv_freq[None, :]
    cos, sin = jnp.cos(ang), jnp.sin(ang)
    return (jnp.concatenate([cos, cos, cos, cos], axis=-1),
            jnp.concatenate([-sin, -sin, sin, sin], axis=-1))


def _slab_gain(g):
    lo, hi = g[:HEAD_DIM // 2], g[HEAD_DIM // 2:]
    return jnp.concatenate([lo, lo, hi, hi]).reshape(1, LANES)


def _slab_columns(width):
    half = HEAD_DIM // 2
    idx = jnp.arange(width).reshape(width // LANES, 2, 2, half)
    return idx.transpose(0, 2, 1, 3).reshape(width)


def kernel(x, mem, norm_mix_g, w_in, pool_w, pool_scale, q_norm_g, k_norm_g, w_out, norm_xattn_g, norm_mem_g, w_xq, w_xkv, xq_norm_g, xk_norm_g, w_xo, norm_ffn_g, w_up, conv_w, conv_b, w_down):
    B, S, D = x.shape
    depth = w_in.shape[0]
    d_ff = w_down.shape[1]
    assert S % 512 == 0 and d_ff % FF_CHUNK == 0
    n_chunks = d_ff // FF_CHUNK
    cos, sin = _rope_tables(S)
    row = lambda a: a.reshape(1, -1)
    pool_width = pool_w.shape[1] * pool_w.shape[2]
    attn_width = (w_in.shape[2] - pool_width) // 3
    perm = _slab_columns(attn_width)
    in_cols = jnp.concatenate([jnp.arange(pool_width), pool_width + perm, pool_width + attn_width + perm,
                               pool_width + 2 * attn_width + jnp.arange(attn_width)])
    for l in range(depth):
        pool, q, k, v, kmean = _in_proj(
            x, row(norm_mix_g[l]), w_in[l][:, in_cols].astype(BF16), pool_w[l].astype(BF16), row(pool_scale[l]),
            _slab_gain(q_norm_g[l]), _slab_gain(k_norm_g[l]), cos, sin, tm=512)
        attn = _moba(q, k, v, kmean.reshape(B, S // MOBA_BLOCK, -1), npairs=2)
        kx, vx = _mem_kv(mem, row(norm_mem_g[l]), w_xkv[l].astype(BF16), row(xk_norm_g[l]))
        x = _mix_xattn(x, pool, attn, w_out[l].astype(BF16), row(norm_xattn_g[l]), w_xq[l].astype(BF16),
                       row(xq_norm_g[l]), kx, vx, w_xo[l].astype(BF16), tm=512)
        w_up_c = w_up[l].astype(BF16).reshape(D, 2 * n_chunks, FF_CHUNK).transpose(1, 0, 2)
        conv_w_c = conv_w[l].reshape(CONV_WIDTH, 2 * n_chunks, FF_CHUNK).transpose(1, 0, 2)
        conv_b_c = conv_b[l].reshape(2 * n_chunks, 1, FF_CHUNK)
        x = _conv_ffn(x, row(norm_ffn_g[l]), w_up_c, conv_w_c, conv_b_c, w_down[l].astype(BF16), tm=512)
    return x
```

```python
import functools

import jax
import jax.numpy as jnp
from jax import lax
from jax.experimental import pallas as pl
from jax.experimental.pallas import tpu as pltpu

F32 = jnp.float32
BF16 = jnp.bfloat16

EPS = 1e-6
LANES = 128
SUBLANES = 8
MXU_COLS = 256
HEAD_DIM = 64
POOL_WINDOWS = (2, 4, 8, 16)
POOL_HALO = SUBLANES * len(POOL_WINDOWS)
MOBA_BLOCK = 256
MOBA_TOPK = 3
ROPE_THETA = 10000.0
XATTN_HEADS = 4
CONV_WIDTH = 3
CONV_HALO = 8
FF_CHUNK = 256
NEG = -1e30
LOG2E = 1.4426950408889634
VMEM_LIMIT = 56 * 1024 * 1024

NT_DIMS = (((1,), (1,)), ((), ()))
TN_DIMS = (((0,), (0,)), ((), ()))


def _rms(x, g):
    return x * lax.rsqrt(jnp.mean(x * x, axis=-1, keepdims=True) + EPS) * g


def _first_head_lanes():
    lane = lax.broadcasted_iota(jnp.int32, (1, LANES), 1)
    return (lane & (HEAD_DIM // 2)) == 0


def _head_norm_rope(y, g, cos, sin_signed):
    first = _first_head_lanes()
    sq = y * y
    s0 = jnp.sum(jnp.where(first, sq, 0.0), axis=-1, keepdims=True)
    s1 = jnp.sum(jnp.where(first, 0.0, sq), axis=-1, keepdims=True)
    r = lax.rsqrt(jnp.where(first, s0, s1) * (1.0 / HEAD_DIM) + EPS)
    yn = y * r * g
    return yn * cos + pltpu.roll(yn, LANES // 2, axis=1) * sin_signed


def _in_proj_kernel(x_ref, g_ref, w_ref, pw_ref, ps_ref, qg_ref, kg_ref, cos_ref, sin_ref,
                    pool_ref, q_ref, k_ref, v_ref, kmean_ref, ubuf_ref, lva_ref, lvb_ref, y_ref,
                    *, tm, pool_width, attn_width):
    s = pl.program_id(1)

    @pl.when(s == 0)
    def _():
        ubuf_ref[0:POOL_HALO, :] = jnp.zeros((POOL_HALO, pool_width), F32)

    hb = _rms(x_ref[0], g_ref[...]).astype(BF16)
    q0, k0, v0 = pool_width, pool_width + attn_width, pool_width + 2 * attn_width

    y_ref[...] = jnp.dot(hb, w_ref[:, q0:v0], preferred_element_type=F32)
    cos = cos_ref[...]
    sin = sin_ref[...]
    qg = qg_ref[...] * (HEAD_DIM ** -0.5 * LOG2E)
    for c in range(attn_width // LANES):
        lanes = slice(c * LANES, (c + 1) * LANES)
        q_ref[0, :, lanes] = _head_norm_rope(y_ref[:, lanes], qg, cos, sin).astype(BF16)
        kr = _head_norm_rope(y_ref[:, attn_width + c * LANES:attn_width + (c + 1) * LANES], kg_ref[...], cos, sin)
        k_ref[0, :, lanes] = kr.astype(BF16)
        for r in range(tm // MOBA_BLOCK):
            km = jnp.sum(kr[r * MOBA_BLOCK:(r + 1) * MOBA_BLOCK], axis=0, keepdims=True) * (1.0 / MOBA_BLOCK)
            kmean_ref[0, r, :, lanes] = km

    u = jnp.dot(hb, w_ref[:, 0:pool_width], preferred_element_type=F32)
    end = POOL_HALO + tm
    ubuf_ref[POOL_HALO:end, :] = u
    gd = pool_width // len(POOL_WINDOWS)
    src = ubuf_ref
    for k in range(1, len(POOL_WINDOWS) + 1):
        dst = lva_ref if k % 2 else lvb_ref
        assert POOL_WINDOWS[k - 1] == 2 ** k
        shift, r0, l0 = 2 ** (k - 1), SUBLANES * k, (k - 1) * gd
        dst[r0:end, l0:] = src[r0:end, l0:] + src[r0 - shift:end - shift, l0:]
        src = dst
    t = s * tm + lax.broadcasted_iota(jnp.int32, (tm, 1), 0)
    for g, w in enumerate(POOL_WINDOWS):
        lanes = slice(g * gd, (g + 1) * gd)
        win = (lva_ref if (g + 1) % 2 else lvb_ref)[POOL_HALO:end, lanes]
        cnt = jnp.minimum(t + 1, w).astype(F32)
        pooled = win / cnt - u[:, lanes]
        mixed = jnp.dot(pooled.astype(BF16), pw_ref[g], preferred_element_type=F32)
        pool_ref[0, :, lanes] = (mixed * ps_ref[:, lanes]).astype(BF16)
    ubuf_ref[0:POOL_HALO, :] = ubuf_ref[tm:end, :]

    v = jnp.dot(hb, w_ref[:, v0:v0 + attn_width], preferred_element_type=F32)
    lane = lax.broadcasted_iota(jnp.int32, (1, LANES), 1)
    for pr in range(attn_width // LANES):
        pair = v[:, pr * LANES:(pr + 1) * LANES]
        for hh in range(2):
            keep = (lane < HEAD_DIM) if hh == 0 else (lane >= HEAD_DIM)
            ones = jnp.where(lane == _ones_lane(hh), 1.0, 0.0)
            h = 2 * pr + hh
            v_ref[0, :, h * LANES:(h + 1) * LANES] = jnp.where(keep, pair, ones).astype(BF16)


def _ones_lane(hh):
    return HEAD_DIM if hh == 0 else 0


def _in_proj(x, g, w_in, pool_w, pool_scale, qg, kg, cos, sin, *, tm):
    B, S, D = x.shape
    n_groups, gd, _ = pool_w.shape
    pool_width = n_groups * gd
    attn_width = (w_in.shape[1] - pool_width) // 3
    nblk = S // MOBA_BLOCK
    kern = functools.partial(_in_proj_kernel, tm=tm, pool_width=pool_width, attn_width=attn_width)
    const2 = lambda b, s: (0, 0)
    tile3 = lambda b, s: (b, s, 0)
    return pl.pallas_call(
        kern,
        grid=(B, S // tm),
        in_specs=[
            pl.BlockSpec((1, tm, D), tile3),
            pl.BlockSpec((1, D), const2),
            pl.BlockSpec(w_in.shape, const2),
            pl.BlockSpec(pool_w.shape, lambda b, s: (0, 0, 0)),
            pl.BlockSpec((1, pool_width), const2),
            pl.BlockSpec((1, LANES), const2),
            pl.BlockSpec((1, LANES), const2),
            pl.BlockSpec((tm, LANES), lambda b, s: (s, 0)),
            pl.BlockSpec((tm, LANES), lambda b, s: (s, 0)),
        ],
        out_specs=[
            pl.BlockSpec((1, tm, pool_width), tile3),
            pl.BlockSpec((1, tm, attn_width), tile3),
            pl.BlockSpec((1, tm, attn_width), tile3),
            pl.BlockSpec((1, tm, 2 * attn_width), tile3),
            pl.BlockSpec((1, tm // MOBA_BLOCK, 1, attn_width), lambda b, s: (b, s, 0, 0)),
        ],
        out_shape=[
            jax.ShapeDtypeStruct((B, S, pool_width), BF16),
            jax.ShapeDtypeStruct((B, S, attn_width), BF16),
            jax.ShapeDtypeStruct((B, S, attn_width), BF16),
            jax.ShapeDtypeStruct((B, S, 2 * attn_width), BF16),
            jax.ShapeDtypeStruct((B, nblk, 1, attn_width), F32),
        ],
        scratch_shapes=[pltpu.VMEM((POOL_HALO + tm, pool_width), F32)] * 3 + [pltpu.VMEM((tm, 2 * attn_width), F32)],
        compiler_params=pltpu.CompilerParams(
            dimension_semantics=("arbitrary", "arbitrary"), vmem_limit_bytes=VMEM_LIMIT),
        name="in_proj",
    )(x, g, w_in, pool_w, pool_scale, qg, kg, cos, sin)


def _moba_kernel(q_ref, k_ref, v_ref, kmean_ref, o_ref, qh_ref, sel_ref, s_ref, m_ref, acc_ref, *, nblk, npairs):
    i = pl.program_id(2)
    blk = MOBA_BLOCK
    first = _first_head_lanes()
    bidx = lax.broadcasted_iota(jnp.int32, (nblk, 1), 0)
    eligible = bidx < i

    for pr in range(npairs):
        lanes = slice(pr * LANES, (pr + 1) * LANES)
        qs = q_ref[0, :, lanes]
        zero = jnp.zeros_like(qs)
        km = kmean_ref[0, :, lanes]
        km_hi = km.astype(BF16)
        km_split = jnp.concatenate([km_hi, (km - km_hi.astype(F32)).astype(BF16)], axis=0)
        for hh in range(2):
            h = 2 * pr + hh
            qm = jnp.where(first, qs, zero) if hh == 0 else jnp.where(first, zero, qs)
            qh_ref[h] = qm.T
            both = jnp.dot(km_split, qh_ref[h], preferred_element_type=F32)
            gate = jnp.where(eligible, both[:nblk] + both[nblk:], -jnp.inf)
            picked = jnp.zeros((nblk, blk), jnp.bool_)
            for _ in range(MOBA_TOPK):
                best = jnp.max(gate, axis=0, keepdims=True)
                where_best = jnp.min(jnp.where(gate == best, bidx, nblk), axis=0, keepdims=True)
                hit = bidx == where_best
                picked = picked | hit
                gate = jnp.where(hit, -jnp.inf, gate)
            sel = jnp.where(picked & eligible, 1.0, 0.0)
            sel_ref[h, 0] = jnp.zeros((1, blk), F32)
            for j in range(nblk):
                sel_ref[h, j + 1] = sel[j:j + 1, :]

    m_ref[...] = jnp.full(m_ref.shape, NEG, F32)
    acc_ref[...] = jnp.zeros_like(acc_ref)

    def scores(slot, kblock):
        off = pl.multiple_of(kblock * blk, blk)
        for pr in range(npairs):
            kb = k_ref[0, pl.ds(off, blk), pr * LANES:(pr + 1) * LANES]
            for hh in range(2):
                h = 2 * pr + hh
                s_ref[slot, h] = jnp.dot(kb, qh_ref[h], preferred_element_type=F32)

    def accumulate(slot, kblock, sel_row):
        off = pl.multiple_of(kblock * blk, blk)
        for pr in range(npairs):
            for hh in range(2):
                h = 2 * pr + hh
                vb = v_ref[0, pl.ds(off, blk), h * LANES:(h + 1) * LANES]
                st = s_ref[slot, h]
                m = m_ref[h]
                if sel_row is None:
                    causal = (lax.broadcasted_iota(jnp.int32, (blk, blk), 0)
                              <= lax.broadcasted_iota(jnp.int32, (blk, blk), 1))
                    st = jnp.where(causal, st, NEG)
                    m_new = jnp.maximum(m, jnp.max(st, axis=0, keepdims=True))
                    p = jnp.exp2(st - m_new)
                else:
                    picked = sel_ref[h, sel_row] > 0.0
                    m_new = jnp.maximum(m, jnp.where(picked, jnp.max(st, axis=0, keepdims=True), NEG))
                    p = jnp.exp2(st - jnp.where(picked, m_new, -NEG))
                alpha = jnp.exp2(m - m_new)
                m_ref[h] = m_new
                pv = lax.dot_general(vb, p.astype(BF16), TN_DIMS, preferred_element_type=F32)
                ones_row = _ones_lane(hh)
                acc_ref[h, 0:HEAD_DIM] = alpha * acc_ref[h, 0:HEAD_DIM] + pv[hh * HEAD_DIM:(hh + 1) * HEAD_DIM]
                acc_ref[h, HEAD_DIM:] = alpha * acc_ref[h, HEAD_DIM:] + pv[ones_row:ones_row + SUBLANES]

    pad = (i + 1) & 1
    n_pairs = (i + 1 + pad) // 2
    scores(0, 0)

    def pair(t, _):
        a = 2 * t - pad
        scores(1, a + 1)
        accumulate(0, jnp.maximum(a, 0), a + 1)
        scores(0, a + 2)
        accumulate(1, a + 1, a + 2)
        return 0

    lax.fori_loop(0, n_pairs - 1, pair, 0)
    scores(1, i)
    accumulate(0, jnp.maximum(i - 1, 0), i)
    accumulate(1, i, None)

    for pr in range(npairs):
        halves = [acc_ref[h, 0:HEAD_DIM] / acc_ref[h, HEAD_DIM:HEAD_DIM + 1] for h in (2 * pr, 2 * pr + 1)]
        o_ref[0, :, pr * LANES:(pr + 1) * LANES] = jnp.concatenate(halves, axis=0).T.astype(BF16)


def _moba(q, k, v, kmean, *, npairs):
    B, S, W = q.shape
    nblk = S // MOBA_BLOCK
    wstep = npairs * LANES
    kern = functools.partial(_moba_kernel, nblk=nblk, npairs=npairs)
    return pl.pallas_call(
        kern,
        grid=(B, W // wstep, nblk),
        in_specs=[
            pl.BlockSpec((1, MOBA_BLOCK, wstep), lambda b, g, i: (b, i, g)),
            pl.BlockSpec((1, S, wstep), lambda b, g, i: (b, 0, g)),
            pl.BlockSpec((1, S, 2 * wstep), lambda b, g, i: (b, 0, g)),
            pl.BlockSpec((1, nblk, wstep), lambda b, g, i: (b, 0, g)),
        ],
        out_specs=pl.BlockSpec((1, MOBA_BLOCK, wstep), lambda b, g, i: (b, i, g)),
        out_shape=jax.ShapeDtypeStruct((B, S, W), BF16),
        scratch_shapes=[pltpu.VMEM((2 * npairs, LANES, MOBA_BLOCK), BF16),
                        pltpu.VMEM((2 * npairs, nblk + 1, 1, MOBA_BLOCK), F32),
                        pltpu.VMEM((2, 2 * npairs, MOBA_BLOCK, MOBA_BLOCK), F32),
                        pltpu.VMEM((2 * npairs, 1, MOBA_BLOCK), F32),
                        pltpu.VMEM((2 * npairs, HEAD_DIM + SUBLANES, MOBA_BLOCK), F32)],
        compiler_params=pltpu.CompilerParams(
            dimension_semantics=("arbitrary", "arbitrary", "arbitrary"), vmem_limit_bytes=VMEM_LIMIT),
        name="moba",
    )(q, k, v, kmean)


def _mem_kv_kernel(mem_ref, g_ref, w_ref, kg_ref, k_ref, v_ref, *, d_model):
    mb = _rms(mem_ref[0], g_ref[...]).astype(BF16)
    hd = d_model // XATTN_HEADS
    for h in range(XATTN_HEADS):
        lanes = slice(h * hd, (h + 1) * hd)
        kh = jnp.dot(mb, w_ref[:, lanes], preferred_element_type=F32)
        k_ref[0, :, lanes] = _rms(kh, kg_ref[...]).astype(BF16)
    v_ref[0] = jnp.dot(mb, w_ref[:, d_model:2 * d_model], preferred_element_type=F32).astype(BF16)


def _mem_kv(mem, g, w_xkv, kg):
    B, M, D = mem.shape
    const2 = lambda b: (0, 0)
    blk3 = lambda b: (b, 0, 0)
    return pl.pallas_call(
        functools.partial(_mem_kv_kernel, d_model=D),
        grid=(B,),
        in_specs=[pl.BlockSpec((1, M, D), blk3), pl.BlockSpec((1, D), const2),
                  pl.BlockSpec(w_xkv.shape, const2), pl.BlockSpec((1, D // XATTN_HEADS), const2)],
        out_specs=[pl.BlockSpec((1, M, D), blk3), pl.BlockSpec((1, M, D), blk3)],
        out_shape=[jax.ShapeDtypeStruct((B, M, D), BF16)] * 2,
        compiler_params=pltpu.CompilerParams(
            dimension_semantics=("arbitrary",), vmem_limit_bytes=VMEM_LIMIT),
        name="mem_kv",
    )(mem, g, w_xkv, kg)


def _mix_xattn_kernel(x_ref, pool_ref, attn_ref, wo_ref, g_ref, wq_ref, qg_ref, k_ref, v_ref, wxo_ref,
                      o_ref, q_ref, qn_ref, s_ref, p_ref, oh_ref, *, pool_width, d_model):
    x1 = (x_ref[0]
          + jnp.dot(pool_ref[0], wo_ref[0:pool_width, :], preferred_element_type=F32)
          + jnp.dot(attn_ref[0], wo_ref[pool_width:, :], preferred_element_type=F32))
    hb = _rms(x1, g_ref[...]).astype(BF16)
    hd = d_model // XATTN_HEADS
    heads = [slice(h * hd, (h + 1) * hd) for h in range(XATTN_HEADS)]
    q_ref[...] = jnp.dot(hb, wq_ref[...], preferred_element_type=F32)
    for lanes in heads:
        qn_ref[:, lanes] = (_rms(q_ref[:, lanes], qg_ref[...]) * (hd ** -0.5)).astype(BF16)
    for h, lanes in enumerate(heads):
        s_ref[h] = lax.dot_general(qn_ref[:, lanes], k_ref[0, :, lanes], NT_DIMS,
                                   preferred_element_type=F32)
    for h in range(XATTN_HEADS):
        s = s_ref[h]
        p = jnp.exp(s - jnp.max(s, axis=-1, keepdims=True))
        p_ref[h] = (p / jnp.sum(p, axis=-1, keepdims=True)).astype(BF16)
    for h, lanes in enumerate(heads):
        oh_ref[:, lanes] = jnp.dot(p_ref[h], v_ref[0, :, lanes], preferred_element_type=F32).astype(BF16)
    o_ref[0] = x1 + jnp.dot(oh_ref[...], wxo_ref[...], preferred_element_type=F32)


def _mix_xattn(x, pool, attn, w_out, g, w_xq, qg, kx, vx, w_xo, *, tm):
    B, S, D = x.shape
    M = kx.shape[1]
    pool_width = pool.shape[2]
    const2 = lambda b, s: (0, 0)
    tile3 = lambda b, s: (b, s, 0)
    mem3 = lambda b, s: (b, 0, 0)
    return pl.pallas_call(
        functools.partial(_mix_xattn_kernel, pool_width=pool_width, d_model=D),
        grid=(B, S // tm),
        in_specs=[
            pl.BlockSpec((1, tm, D), tile3),
            pl.BlockSpec((1, tm, pool_width), tile3),
            pl.BlockSpec((1, tm, attn.shape[2]), tile3),
            pl.BlockSpec(w_out.shape, const2),
            pl.BlockSpec((1, D), const2),
            pl.BlockSpec(w_xq.shape, const2),
            pl.BlockSpec((1, D // XATTN_HEADS), const2),
            pl.BlockSpec((1, M, D), mem3),
            pl.BlockSpec((1, M, D), mem3),
            pl.BlockSpec(w_xo.shape, const2),
        ],
        out_specs=pl.BlockSpec((1, tm, D), tile3),
        out_shape=jax.ShapeDtypeStruct((B, S, D), F32),
        scratch_shapes=[pltpu.VMEM((tm, D), F32), pltpu.VMEM((tm, D), BF16),
                        pltpu.VMEM((XATTN_HEADS, tm, M), F32), pltpu.VMEM((XATTN_HEADS, tm, M), BF16),
                        pltpu.VMEM((tm, D), BF16)],
        compiler_params=pltpu.CompilerParams(
            dimension_semantics=("arbitrary", "arbitrary"), vmem_limit_bytes=VMEM_LIMIT),
        name="mix_xattn",
    )(x, pool, attn, w_out, g, w_xq, qg, kx, vx, w_xo)


def _conv_ffn_kernel(x_ref, g_ref, wu_ref, cw_ref, cb_ref, wd_ref, o_ref,
                     hb_ref, halo_ref, buf_ref, act_ref, *, tm, n_chunks):
    s = pl.program_id(1)

    @pl.when(s == 0)
    def _():
        halo_ref[...] = jnp.zeros_like(halo_ref)

    x = x_ref[0]
    hb_ref[...] = _rms(x, g_ref[...]).astype(BF16)

    def conv(c, slot):
        cols = slice(c * FF_CHUNK, (c + 1) * FF_CHUNK)
        up = jnp.dot(hb_ref[...], wu_ref[:, cols], preferred_element_type=F32)
        buf_ref[slot, 0:CONV_HALO, :] = halo_ref[c]
        buf_ref[slot, CONV_HALO:CONV_HALO + tm, :] = up
        halo_ref[c] = up[tm - CONV_HALO:, :]
        cw = cw_ref[:, cols]
        y = up * cw[CONV_WIDTH - 1:CONV_WIDTH, :] + cb_ref[:, cols]
        for d in range(1, CONV_WIDTH):
            y = y + buf_ref[slot, CONV_HALO - d:CONV_HALO - d + tm, :] * cw[CONV_WIDTH - 1 - d:CONV_WIDTH - d, :]
        return y

    for c in range(n_chunks):
        gate = conv(c, 2 * (c % 2))
        val = conv(c + n_chunks, 2 * (c % 2) + 1)
        act_ref[:, c * FF_CHUNK:(c + 1) * FF_CHUNK] = (gate * (1.0 / (1.0 + jnp.exp(-gate))) * val).astype(BF16)
    o_ref[0] = x + jnp.dot(act_ref[...], wd_ref[...], preferred_element_type=F32)


def _conv_ffn(x, g, w_up, conv_w, conv_b, w_down, *, tm):
    B, S, D = x.shape
    d_ff = w_down.shape[0]
    n_chunks = d_ff // FF_CHUNK
    const2 = lambda b, s: (0, 0)
    tile3 = lambda b, s: (b, s, 0)
    return pl.pallas_call(
        functools.partial(_conv_ffn_kernel, tm=tm, n_chunks=n_chunks),
        grid=(B, S // tm),
        in_specs=[
            pl.BlockSpec((1, tm, D), tile3),
            pl.BlockSpec((1, D), const2),
            pl.BlockSpec(w_up.shape, const2),
            pl.BlockSpec(conv_w.shape, const2),
            pl.BlockSpec(conv_b.shape, const2),
            pl.BlockSpec(w_down.shape, const2),
        ],
        out_specs=pl.BlockSpec((1, tm, D), tile3),
        out_shape=jax.ShapeDtypeStruct((B, S, D), F32),
        scratch_shapes=[
            pltpu.VMEM((tm, D), BF16),
            pltpu.VMEM((2 * n_chunks, CONV_HALO, FF_CHUNK), F32),
            pltpu.VMEM((4, CONV_HALO + tm, FF_CHUNK), F32),
            pltpu.VMEM((tm, d_ff), BF16),
        ],
        compiler_params=pltpu.CompilerParams(
            dimension_semantics=("arbitrary", "arbitrary"), vmem_limit_bytes=VMEM_LIMIT),
        name="conv_ffn",
    )(x, g, w_up, conv_w, conv_b, w_down)


def _rope_tables(S):
    half = HEAD_DIM // 2
    inv_freq = ROPE_THETA ** (-jnp.arange(half, dtype=F32) / half)
    ang = jnp.arange(S).astype(F32)[:, None] * inv_freq[None, :]
    cos, sin = jnp.cos(ang), jnp.sin(ang)
    return (jnp.concatenate([cos, cos, cos, cos], axis=-1),
            jnp.concatenate([-sin, -sin, sin, sin], axis=-1))


def _slab_gain(g):
    lo, hi = g[:HEAD_DIM // 2], g[HEAD_DIM // 2:]
    return jnp.concatenate([lo, lo, hi, hi]).reshape(1, LANES)


def _slab_columns(w):
    rows, width = w.shape
    w = w.reshape(rows, width // LANES, 2, 2, HEAD_DIM // 2)
    return w.transpose(0, 1, 3, 2, 4).reshape(rows, width)


def kernel(x, mem, norm_mix_g, w_in, pool_w, pool_scale, q_norm_g, k_norm_g, w_out, norm_xattn_g, norm_mem_g, w_xq, w_xkv, xq_norm_g, xk_norm_g, w_xo, norm_ffn_g, w_up, conv_w, conv_b, w_down):
    B, S, D = x.shape
    depth = w_in.shape[0]
    d_ff = w_down.shape[1]
    assert S % 512 == 0 and d_ff % FF_CHUNK == 0
    cos, sin = _rope_tables(S)
    row = lambda a: a.reshape(1, -1)
    pool_width = pool_w.shape[1] * pool_w.shape[2]
    attn_width = (w_in.shape[2] - pool_width) // 3
    q0, k0, v0 = pool_width, pool_width + attn_width, pool_width + 2 * attn_width
    for l in range(depth):
        w_in_l = jnp.concatenate([w_in[l][:, :q0], _slab_columns(w_in[l][:, q0:k0]),
                                  _slab_columns(w_in[l][:, k0:v0]), w_in[l][:, v0:]], axis=1).astype(BF16)
        pool, q, k, v, kmean = _in_proj(
            x, row(norm_mix_g[l]), w_in_l, pool_w[l].astype(BF16), row(pool_scale[l]),
            _slab_gain(q_norm_g[l]), _slab_gain(k_norm_g[l]), cos, sin, tm=512)
        attn = _moba(q, k, v, kmean.reshape(B, S // MOBA_BLOCK, -1), npairs=4)
        kx, vx = _mem_kv(mem, row(norm_mem_g[l]), w_xkv[l].astype(BF16), row(xk_norm_g[l]))
        x = _mix_xattn(x, pool, attn, w_out[l].astype(BF16), row(norm_xattn_g[l]), w_xq[l].astype(BF16),
                       row(xq_norm_g[l]), kx, vx, w_xo[l].astype(BF16), tm=512)
        x = _conv_ffn(x, row(norm_ffn_g[l]), w_up[l].astype(BF16), conv_w[l], row(conv_b[l]), w_down[l].astype(BF16),
                      tm=512)
    return x
```

```python
import functools

import jax
import jax.numpy as jnp
from jax import lax
from jax.experimental import pallas as pl
from jax.experimental.pallas import tpu as pltpu

F32 = jnp.float32
BF16 = jnp.bfloat16

EPS = 1e-6
LANES = 128
SUBLANES = 8
MXU_COLS = 256
HEAD_DIM = 64
POOL_WINDOWS = (2, 4, 8, 16)
POOL_HALO = SUBLANES * len(POOL_WINDOWS)
MOBA_BLOCK = 256
MOBA_TOPK = 3
ROPE_THETA = 10000.0
XATTN_HEADS = 4
CONV_WIDTH = 3
CONV_HALO = 8
FF_CHUNK = 256
NEG = -1e30
LOG2E = 1.4426950408889634
VMEM_LIMIT = 56 * 1024 * 1024

NT_DIMS = (((1,), (1,)), ((), ()))
TN_DIMS = (((0,), (0,)), ((), ()))


def _rms(x, g):
    return x * lax.rsqrt(jnp.mean(x * x, axis=-1, keepdims=True) + EPS) * g


def _first_head_lanes():
    lane = lax.broadcasted_iota(jnp.int32, (1, LANES), 1)
    return (lane & (HEAD_DIM // 2)) == 0


def _head_norm_rope(y, g, cos, sin_signed):
    first = _first_head_lanes()
    sq = y * y
    s0 = jnp.sum(jnp.where(first, sq, 0.0), axis=-1, keepdims=True)
    s1 = jnp.sum(jnp.where(first, 0.0, sq), axis=-1, keepdims=True)
    r = lax.rsqrt(jnp.where(first, s0, s1) * (1.0 / HEAD_DIM) + EPS)
    yn = y * r * g
    return yn * cos + pltpu.roll(yn, LANES // 2, axis=1) * sin_signed


def _in_proj_kernel(x_ref, g_ref, w_ref, pw_ref, ps_ref, qg_ref, kg_ref, cos_ref, sin_ref,
                    pool_ref, q_ref, k_ref, v_ref, kmean_ref, ubuf_ref, lva_ref, lvb_ref, y_ref,
                    *, tm, pool_width, attn_width):
    s = pl.program_id(1)

    @pl.when(s == 0)
    def _():
        ubuf_ref[0:POOL_HALO, :] = jnp.zeros((POOL_HALO, pool_width), F32)

    hb = _rms(x_ref[0], g_ref[...]).astype(BF16)
    q0, k0, v0 = pool_width, pool_width + attn_width, pool_width + 2 * attn_width

    y_ref[...] = jnp.dot(hb, w_ref[:, q0:v0], preferred_element_type=F32)
    cos = cos_ref[...]
    sin = sin_ref[...]
    qg = qg_ref[...] * (HEAD_DIM ** -0.5 * LOG2E)
    for c in range(attn_width // LANES):
        lanes = slice(c * LANES, (c + 1) * LANES)
        q_ref[0, :, lanes] = _head_norm_rope(y_ref[:, lanes], qg, cos, sin).astype(BF16)
        kr = _head_norm_rope(y_ref[:, attn_width + c * LANES:attn_width + (c + 1) * LANES], kg_ref[...], cos, sin)
        k_ref[0, :, lanes] = kr.astype(BF16)
        for r in range(tm // MOBA_BLOCK):
            km = jnp.sum(kr[r * MOBA_BLOCK:(r + 1) * MOBA_BLOCK], axis=0, keepdims=True) * (1.0 / MOBA_BLOCK)
            kmean_ref[0, r, :, lanes] = km

    u = jnp.dot(hb, w_ref[:, 0:pool_width], preferred_element_type=F32)
    end = POOL_HALO + tm
    ubuf_ref[POOL_HALO:end, :] = u
    gd = pool_width // len(POOL_WINDOWS)
    src = ubuf_ref
    for k in range(1, len(POOL_WINDOWS) + 1):
        dst = lva_ref if k % 2 else lvb_ref
        assert POOL_WINDOWS[k - 1] == 2 ** k
        shift, r0, l0 = 2 ** (k - 1), SUBLANES * k, (k - 1) * gd
        dst[r0:end, l0:] = src[r0:end, l0:] + src[r0 - shift:end - shift, l0:]
        src = dst
    t = s * tm + lax.broadcasted_iota(jnp.int32, (tm, 1), 0)
    for g, w in enumerate(POOL_WINDOWS):
        lanes = slice(g * gd, (g + 1) * gd)
        win = (lva_ref if (g + 1) % 2 else lvb_ref)[POOL_HALO:end, lanes]
        cnt = jnp.minimum(t + 1, w).astype(F32)
        pooled = win / cnt - u[:, lanes]
        mixed = jnp.dot(pooled.astype(BF16), pw_ref[g], preferred_element_type=F32)
        pool_ref[0, :, lanes] = (mixed * ps_ref[:, lanes]).astype(BF16)
    ubuf_ref[0:POOL_HALO, :] = ubuf_ref[tm:end, :]

    v = jnp.dot(hb, w_ref[:, v0:v0 + attn_width], preferred_element_type=F32)
    lane = lax.broadcasted_iota(jnp.int32, (1, LANES), 1)
    for pr in range(attn_width // LANES):
        pair = v[:, pr * LANES:(pr + 1) * LANES]
        for hh in range(2):
            keep = (lane < HEAD_DIM) if hh == 0 else (lane >= HEAD_DIM)
            ones = jnp.where(lane == _ones_lane(hh), 1.0, 0.0)
            h = 2 * pr + hh
            v_ref[0, :, h * LANES:(h + 1) * LANES] = jnp.where(keep, pair, ones).astype(BF16)


def _ones_lane(hh):
    return HEAD_DIM if hh == 0 else 0


def _in_proj(x, g, w_in, pool_w, pool_scale, qg, kg, cos, sin, *, tm):
    B, S, D = x.shape
    n_groups, gd, _ = pool_w.shape
    pool_width = n_groups * gd
    attn_width = (w_in.shape[1] - pool_width) // 3
    nblk = S // MOBA_BLOCK
    kern = functools.partial(_in_proj_kernel, tm=tm, pool_width=pool_width, attn_width=attn_width)
    const2 = lambda b, s: (0, 0)
    tile3 = lambda b, s: (b, s, 0)
    return pl.pallas_call(
        kern,
        grid=(B, S // tm),
        in_specs=[
            pl.BlockSpec((1, tm, D), tile3),
            pl.BlockSpec((1, D), const2),
            pl.BlockSpec(w_in.shape, const2),
            pl.BlockSpec(pool_w.shape, lambda b, s: (0, 0, 0)),
            pl.BlockSpec((1, pool_width), const2),
            pl.BlockSpec((1, LANES), const2),
            pl.BlockSpec((1, LANES), const2),
            pl.BlockSpec((tm, LANES), lambda b, s: (s, 0)),
            pl.BlockSpec((tm, LANES), lambda b, s: (s, 0)),
        ],
        out_specs=[
            pl.BlockSpec((1, tm, pool_width), tile3),
            pl.BlockSpec((1, tm, attn_width), tile3),
            pl.BlockSpec((1, tm, attn_width), tile3),
            pl.BlockSpec((1, tm, 2 * attn_width), tile3),
            pl.BlockSpec((1, tm // MOBA_BLOCK, 1, attn_width), lambda b, s: (b, s, 0, 0)),
        ],
        out_shape=[
            jax.ShapeDtypeStruct((B, S, pool_width), BF16),
            jax.ShapeDtypeStruct((B, S, attn_width), BF16),
            jax.ShapeDtypeStruct((B, S, attn_width), BF16),
            jax.ShapeDtypeStruct((B, S, 2 * attn_width), BF16),
            jax.ShapeDtypeStruct((B, nblk, 1, attn_width), F32),
        ],
        scratch_shapes=[pltpu.VMEM((POOL_HALO + tm, pool_width), F32)] * 3 + [pltpu.VMEM((tm, 2 * attn_width), F32)],
        compiler_params=pltpu.CompilerParams(
            dimension_semantics=("arbitrary", "arbitrary"), vmem_limit_bytes=VMEM_LIMIT),
        name="in_proj",
    )(x, g, w_in, pool_w, pool_scale, qg, kg, cos, sin)


def _moba_kernel(q_ref, k_ref, v_ref, kmean_ref, o_ref, qh_ref, sel_ref, s_ref, m_ref, acc_ref, *, nblk, npairs):
    i = pl.program_id(2)
    blk = MOBA_BLOCK
    first = _first_head_lanes()
    bidx = lax.broadcasted_iota(jnp.int32, (nblk, 1), 0)
    eligible = bidx < i

    for pr in range(npairs):
        lanes = slice(pr * LANES, (pr + 1) * LANES)
        qs = q_ref[0, :, lanes]
        zero = jnp.zeros_like(qs)
        km = kmean_ref[0, :, lanes]
        km_hi = km.astype(BF16)
        km_split = jnp.concatenate([km_hi, (km - km_hi.astype(F32)).astype(BF16)], axis=0)
        for hh in range(2):
            h = 2 * pr + hh
            qm = jnp.where(first, qs, zero) if hh == 0 else jnp.where(first, zero, qs)
            qh_ref[h] = qm.T
            both = jnp.dot(km_split, qh_ref[h], preferred_element_type=F32)
            gate = jnp.where(eligible, both[:nblk] + both[nblk:], -jnp.inf)
            picked = jnp.zeros((nblk, blk), jnp.bool_)
            for _ in range(MOBA_TOPK):
                best = jnp.max(gate, axis=0, keepdims=True)
                where_best = jnp.min(jnp.where(gate == best, bidx, nblk), axis=0, keepdims=True)
                hit = bidx == where_best
                picked = picked | hit
                gate = jnp.where(hit, -jnp.inf, gate)
            sel = jnp.where(picked & eligible, 1.0, 0.0)
            for j in range(nblk):
                sel_ref[h, j] = sel[j:j + 1, :]

    m_ref[...] = jnp.full(m_ref.shape, NEG, F32)
    acc_ref[...] = jnp.zeros_like(acc_ref)

    def scores(slot, kblock):
        off = pl.multiple_of(kblock * blk, blk)
        for pr in range(npairs):
            kb = k_ref[0, pl.ds(off, blk), pr * LANES:(pr + 1) * LANES]
            for hh in range(2):
                h = 2 * pr + hh
                s_ref[slot, h] = jnp.dot(kb, qh_ref[h], preferred_element_type=F32)

    def accumulate(slot, kblock, past):
        off = pl.multiple_of(kblock * blk, blk)
        for pr in range(npairs):
            for hh in range(2):
                h = 2 * pr + hh
                vb = v_ref[0, pl.ds(off, blk), h * LANES:(h + 1) * LANES]
                st = s_ref[slot, h]
                m = m_ref[h]
                if not past:
                    causal = (lax.broadcasted_iota(jnp.int32, (blk, blk), 0)
                              <= lax.broadcasted_iota(jnp.int32, (blk, blk), 1))
                    st = jnp.where(causal, st, NEG)
                    m_new = jnp.maximum(m, jnp.max(st, axis=0, keepdims=True))
                    p = jnp.exp2(st - m_new)
                else:
                    picked = sel_ref[h, kblock] > 0.0
                    m_new = jnp.maximum(m, jnp.where(picked, jnp.max(st, axis=0, keepdims=True), NEG))
                    p = jnp.exp2(st - jnp.where(picked, m_new, -NEG))
                alpha = jnp.exp2(m - m_new)
                m_ref[h] = m_new
                pv = lax.dot_general(vb, p.astype(BF16), TN_DIMS, preferred_element_type=F32)
                ones_row = _ones_lane(hh)
                acc_ref[h, 0:HEAD_DIM] = alpha * acc_ref[h, 0:HEAD_DIM] + pv[hh * HEAD_DIM:(hh + 1) * HEAD_DIM]
                acc_ref[h, HEAD_DIM:] = alpha * acc_ref[h, HEAD_DIM:] + pv[ones_row:ones_row + SUBLANES]

    scores(0, 0)

    def pair(t, _):
        a = 2 * t
        scores(1, a + 1)
        accumulate(0, a, True)
        scores(0, a + 2)
        accumulate(1, a + 1, True)
        return 0

    lax.fori_loop(0, i // 2, pair, 0)

    @pl.when(i % 2 == 0)
    def _():
        accumulate(0, i, False)

    @pl.when(i % 2 == 1)
    def _():
        scores(1, i)
        accumulate(0, i - 1, True)
        accumulate(1, i, False)

    for pr in range(npairs):
        halves = [acc_ref[h, 0:HEAD_DIM] / acc_ref[h, HEAD_DIM:HEAD_DIM + 1] for h in (2 * pr, 2 * pr + 1)]
        o_ref[0, :, pr * LANES:(pr + 1) * LANES] = jnp.concatenate(halves, axis=0).T.astype(BF16)


def _moba(q, k, v, kmean, *, npairs):
    B, S, W = q.shape
    nblk = S // MOBA_BLOCK
    wstep = npairs * LANES
    kern = functools.partial(_moba_kernel, nblk=nblk, npairs=npairs)
    return pl.pallas_call(
        kern,
        grid=(B, W // wstep, nblk),
        in_specs=[
            pl.BlockSpec((1, MOBA_BLOCK, wstep), lambda b, g, i: (b, i, g)),
            pl.BlockSpec((1, S, wstep), lambda b, g, i: (b, 0, g)),
            pl.BlockSpec((1, S, 2 * wstep), lambda b, g, i: (b, 0, g)),
            pl.BlockSpec((1, nblk, wstep), lambda b, g, i: (b, 0, g)),
        ],
        out_specs=pl.BlockSpec((1, MOBA_BLOCK, wstep), lambda b, g, i: (b, i, g)),
        out_shape=jax.ShapeDtypeStruct((B, S, W), BF16),
        scratch_shapes=[pltpu.VMEM((2 * npairs, LANES, MOBA_BLOCK), BF16),
                        pltpu.VMEM((2 * npairs, nblk, 1, MOBA_BLOCK), F32),
                        pltpu.VMEM((2, 2 * npairs, MOBA_BLOCK, MOBA_BLOCK), F32),
                        pltpu.VMEM((2 * npairs, 1, MOBA_BLOCK), F32),
                        pltpu.VMEM((2 * npairs, HEAD_DIM + SUBLANES, MOBA_BLOCK), F32)],
        compiler_params=pltpu.CompilerParams(
            dimension_semantics=("arbitrary", "arbitrary", "arbitrary"), vmem_limit_bytes=VMEM_LIMIT),
        name="moba",
    )(q, k, v, kmean)


def _mem_kv_kernel(mem_ref, g_ref, w_ref, kg_ref, k_ref, v_ref, *, d_model):
    mb = _rms(mem_ref[0], g_ref[...]).astype(BF16)
    hd = d_model // XATTN_HEADS
    for h in range(XATTN_HEADS):
        lanes = slice(h * hd, (h + 1) * hd)
        kh = jnp.dot(mb, w_ref[:, lanes], preferred_element_type=F32)
        k_ref[0, :, lanes] = _rms(kh, kg_ref[...]).astype(BF16)
    v_ref[0] = jnp.dot(mb, w_ref[:, d_model:2 * d_model], preferred_element_type=F32).astype(BF16)


def _mem_kv(mem, g, w_xkv, kg):
    B, M, D = mem.shape
    const2 = lambda b: (0, 0)
    blk3 = lambda b: (b, 0, 0)
    return pl.pallas_call(
        functools.partial(_mem_kv_kernel, d_model=D),
        grid=(B,),
        in_specs=[pl.BlockSpec((1, M, D), blk3), pl.BlockSpec((1, D), const2),
                  pl.BlockSpec(w_xkv.shape, const2), pl.BlockSpec((1, D // XATTN_HEADS), const2)],
        out_specs=[pl.BlockSpec((1, M, D), blk3), pl.BlockSpec((1, M, D), blk3)],
        out_shape=[jax.ShapeDtypeStruct((B, M, D), BF16)] * 2,
        compiler_params=pltpu.CompilerParams(
            dimension_semantics=("arbitrary",), vmem_limit_bytes=VMEM_LIMIT),
        name="mem_kv",
    )(mem, g, w_xkv, kg)


def _mix_xattn_kernel(x_ref, pool_ref, attn_ref, wo_ref, g_ref, wq_ref, qg_ref, k_ref, v_ref, wxo_ref,
                      o_ref, q_ref, qn_ref, s_ref, p_ref, oh_ref, *, pool_width, d_model):
    x1 = (x_ref[0]
          + jnp.dot(pool_ref[0], wo_ref[0:pool_width, :], preferred_element_type=F32)
          + jnp.dot(attn_ref[0], wo_ref[pool_width:, :], preferred_element_type=F32))
    hb = _rms(x1, g_ref[...]).astype(BF16)
    hd = d_model // XATTN_HEADS
    heads = [slice(h * hd, (h + 1) * hd) for h in range(XATTN_HEADS)]
    q_ref[...] = jnp.dot(hb, wq_ref[...], preferred_element_type=F32)
    for lanes in heads:
        qn_ref[:, lanes] = (_rms(q_ref[:, lanes], qg_ref[...]) * (hd ** -0.5)).astype(BF16)
    for h, lanes in enumerate(heads):
        s_ref[h] = lax.dot_general(qn_ref[:, lanes], k_ref[0, :, lanes], NT_DIMS,
                                   preferred_element_type=F32)
    for h in range(XATTN_HEADS):
        s = s_ref[h]
        p = jnp.exp(s - jnp.max(s, axis=-1, keepdims=True))
        p_ref[h] = (p / jnp.sum(p, axis=-1, keepdims=True)).astype(BF16)
    for h, lanes in enumerate(heads):
        oh_ref[:, lanes] = jnp.dot(p_ref[h], v_ref[0, :, lanes], preferred_element_type=F32).astype(BF16)
    o_ref[0] = x1 + jnp.dot(oh_ref[...], wxo_ref[...], preferred_element_type=F32)


def _mix_xattn(x, pool, attn, w_out, g, w_xq, qg, kx, vx, w_xo, *, tm):
    B, S, D = x.shape
    M = kx.shape[1]
    pool_width = pool.shape[2]
    const2 = lambda b, s: (0, 0)
    tile3 = lambda b, s: (b, s, 0)
    mem3 = lambda b, s: (b, 0, 0)
    return pl.pallas_call(
        functools.partial(_mix_xattn_kernel, pool_width=pool_width, d_model=D),
        grid=(B, S // tm),
        in_specs=[
            pl.BlockSpec((1, tm, D), tile3),
            pl.BlockSpec((1, tm, pool_width), tile3),
            pl.BlockSpec((1, tm, attn.shape[2]), tile3),
            pl.BlockSpec(w_out.shape, const2),
            pl.BlockSpec((1, D), const2),
            pl.BlockSpec(w_xq.shape, const2),
            pl.BlockSpec((1, D // XATTN_HEADS), const2),
            pl.BlockSpec((1, M, D), mem3),
            pl.BlockSpec((1, M, D), mem3),
            pl.BlockSpec(w_xo.shape, const2),
        ],
        out_specs=pl.BlockSpec((1, tm, D), tile3),
        out_shape=jax.ShapeDtypeStruct((B, S, D), F32),
        scratch_shapes=[pltpu.VMEM((tm, D), F32), pltpu.VMEM((tm, D), BF16),
                        pltpu.VMEM((XATTN_HEADS, tm, M), F32), pltpu.VMEM((XATTN_HEADS, tm, M), BF16),
                        pltpu.VMEM((tm, D), BF16)],
        compiler_params=pltpu.CompilerParams(
            dimension_semantics=("arbitrary", "arbitrary"), vmem_limit_bytes=VMEM_LIMIT),
        name="mix_xattn",
    )(x, pool, attn, w_out, g, w_xq, qg, kx, vx, w_xo)


def _conv_ffn_kernel(x_ref, g_ref, wu_ref, cw_ref, cb_ref, wd_ref, o_ref,
                     hb_ref, halo_ref, buf_ref, act_ref, *, tm, n_chunks):
    s = pl.program_id(1)

    @pl.when(s == 0)
    def _():
        halo_ref[...] = jnp.zeros_like(halo_ref)

    x = x_ref[0]
    hb_ref[...] = _rms(x, g_ref[...]).astype(BF16)

    def conv(c, slot):
        cols = slice(c * FF_CHUNK, (c + 1) * FF_CHUNK)
        up = jnp.dot(hb_ref[...], wu_ref[:, cols], preferred_element_type=F32)
        buf_ref[slot, 0:CONV_HALO, :] = halo_ref[c]
        buf_ref[slot, CONV_HALO:CONV_HALO + tm, :] = up
        halo_ref[c] = up[tm - CONV_HALO:, :]
        cw = cw_ref[:, cols]
        y = up * cw[CONV_WIDTH - 1:CONV_WIDTH, :] + cb_ref[:, cols]
        for d in range(1, CONV_WIDTH):
            y = y + buf_ref[slot, CONV_HALO - d:CONV_HALO - d + tm, :] * cw[CONV_WIDTH - 1 - d:CONV_WIDTH - d, :]
        return y

    for c in range(n_chunks):
        gate = conv(c, 2 * (c % 2))
        val = conv(c + n_chunks, 2 * (c % 2) + 1)
        act_ref[:, c * FF_CHUNK:(c + 1) * FF_CHUNK] = (gate * (1.0 / (1.0 + jnp.exp(-gate))) * val).astype(BF16)
    o_ref[0] = x + jnp.dot(act_ref[...], wd_ref[...], preferred_element_type=F32)


def _conv_ffn(x, g, w_up, conv_w, conv_b, w_down, *, tm):
    B, S, D = x.shape
    d_ff = w_down.shape[0]
    n_chunks = d_ff // FF_CHUNK
    const2 = lambda b, s: (0, 0)
    tile3 = lambda b, s: (b, s, 0)
    return pl.pallas_call(
        functools.partial(_conv_ffn_kernel, tm=tm, n_chunks=n_chunks),
        grid=(B, S // tm),
        in_specs=[
            pl.BlockSpec((1, tm, D), tile3),
            pl.BlockSpec((1, D), const2),
            pl.BlockSpec(w_up.shape, const2),
            pl.BlockSpec(conv_w.shape, const2),
            pl.BlockSpec(conv_b.shape, const2),
            pl.BlockSpec(w_down.shape, const2),
        ],
        out_specs=pl.BlockSpec((1, tm, D), tile3),
        out_shape=jax.ShapeDtypeStruct((B, S, D), F32),
        scratch_shapes=[
            pltpu.VMEM((tm, D), BF16),
            pltpu.VMEM((2 * n_chunks, CONV_HALO, FF_CHUNK), F32),
            pltpu.VMEM((4, CONV_HALO + tm, FF_CHUNK), F32),
            pltpu.VMEM((tm, d_ff), BF16),
        ],
        compiler_params=pltpu.CompilerParams(
            dimension_semantics=("arbitrary", "arbitrary"), vmem_limit_bytes=VMEM_LIMIT),
        name="conv_ffn",
    )(x, g, w_up, conv_w, conv_b, w_down)


def _rope_tables(S):
    half = HEAD_DIM // 2
    inv_freq = ROPE_THETA ** (-jnp.arange(half, dtype=F32) / half)
    ang = jnp.arange(S).astype(F32)[:, None] * inv_freq[None, :]
    cos, sin = jnp.cos(ang), jnp.sin(ang)
    return (jnp.concatenate([cos, cos, cos, cos], axis=-1),
            jnp.concatenate([-sin, -sin, sin, sin], axis=-1))


def _slab_gain(g):
    lo, hi = g[:HEAD_DIM // 2], g[HEAD_DIM // 2:]
    return jnp.concatenate([lo, lo, hi, hi]).reshape(1, LANES)


def _slab_columns(w):
    rows, width = w.shape
    w = w.reshape(rows, width // LANES, 2, 2, HEAD_DIM // 2)
    return w.transpose(0, 1, 3, 2, 4).reshape(rows, width)


def kernel(x, mem, norm_mix_g, w_in, pool_w, pool_scale, q_norm_g, k_norm_g, w_out, norm_xattn_g, norm_mem_g, w_xq, w_xkv, xq_norm_g, xk_norm_g, w_xo, norm_ffn_g, w_up, conv_w, conv_b, w_down):
    B, S, D = x.shape
    depth = w_in.shape[0]
    d_ff = w_down.shape[1]
    assert S % 512 == 0 and d_ff % FF_CHUNK == 0
    cos, sin = _rope_tables(S)
    row = lambda a: a.reshape(1, -1)
    pool_width = pool_w.shape[1] * pool_w.shape[2]
    attn_width = (w_in.shape[2] - pool_width) // 3
    q0, k0, v0 = pool_width, pool_width + attn_width, pool_width + 2 * attn_width
    for l in range(depth):
        w_in_l = jnp.concatenate([w_in[l][:, :q0], _slab_columns(w_in[l][:, q0:k0]),
                                  _slab_columns(w_in[l][:, k0:v0]), w_in[l][:, v0:]], axis=1).astype(BF16)
        pool, q, k, v, kmean = _in_proj(
            x, row(norm_mix_g[l]), w_in_l, pool_w[l].astype(BF16), row(pool_scale[l]),
            _slab_gain(q_norm_g[l]), _slab_gain(k_norm_g[l]), cos, sin, tm=512)
        attn = _moba(q, k, v, kmean.reshape(B, S // MOBA_BLOCK, -1), npairs=4)
        kx, vx = _mem_kv(mem, row(norm_mem_g[l]), w_xkv[l].astype(BF16), row(xk_norm_g[l]))
        x = _mix_xattn(x, pool, attn, w_out[l].astype(BF16), row(norm_xattn_g[l]), w_xq[l].astype(BF16),
                       row(xq_norm_g[l]), kx, vx, w_xo[l].astype(BF16), tm=512)
        x = _conv_ffn(x, row(norm_ffn_g[l]), w_up[l].astype(BF16), conv_w[l], row(conv_b[l]), w_down[l].astype(BF16),
                      tm=512)
    return x
```

```python
import functools

import jax
import jax.numpy as jnp
from jax import lax
from jax.experimental import pallas as pl
from jax.experimental.pallas import tpu as pltpu

F32 = jnp.float32
BF16 = jnp.bfloat16

EPS = 1e-6
LANES = 128
SUBLANES = 8
MXU_COLS = 256
HEAD_DIM = 64
POOL_WINDOWS = (2, 4, 8, 16)
POOL_HALO = SUBLANES * len(POOL_WINDOWS)
MOBA_BLOCK = 256
MOBA_TOPK = 3
ROPE_THETA = 10000.0
XATTN_HEADS = 4
CONV_WIDTH = 3
CONV_HALO = 8
FF_CHUNK = 256
NEG = -1e30
LOG2E = 1.4426950408889634
VMEM_LIMIT = 56 * 1024 * 1024

NT_DIMS = (((1,), (1,)), ((), ()))
TN_DIMS = (((0,), (0,)), ((), ()))


def _rms(x, g):
    return x * lax.rsqrt(jnp.mean(x * x, axis=-1, keepdims=True) + EPS) * g


def _first_head_lanes():
    lane = lax.broadcasted_iota(jnp.int32, (1, LANES), 1)
    return (lane & (HEAD_DIM // 2)) == 0


def _head_norm_rope(y, g, cos, sin_signed):
    first = _first_head_lanes()
    sq = y * y
    s0 = jnp.sum(jnp.where(first, sq, 0.0), axis=-1, keepdims=True)
    s1 = jnp.sum(jnp.where(first, 0.0, sq), axis=-1, keepdims=True)
    r = lax.rsqrt(jnp.where(first, s0, s1) * (1.0 / HEAD_DIM) + EPS)
    yn = y * r * g
    return yn * cos + pltpu.roll(yn, LANES // 2, axis=1) * sin_signed


def _in_proj_kernel(x_ref, g_ref, w_ref, pw_ref, ps_ref, qg_ref, kg_ref, cos_ref, sin_ref,
                    pool_ref, q_ref, k_ref, v_ref, kmean_ref, ubuf_ref, lva_ref, lvb_ref, y_ref,
                    *, tm, pool_width, attn_width):
    s = pl.program_id(1)

    @pl.when(s == 0)
    def _():
        ubuf_ref[0:POOL_HALO, :] = jnp.zeros((POOL_HALO, pool_width), F32)

    hb = _rms(x_ref[0], g_ref[...]).astype(BF16)
    q0, k0, v0 = pool_width, pool_width + attn_width, pool_width + 2 * attn_width

    y_ref[...] = jnp.dot(hb, w_ref[:, q0:v0], preferred_element_type=F32)
    cos = cos_ref[...]
    sin = sin_ref[...]
    qg = qg_ref[...] * (HEAD_DIM ** -0.5 * LOG2E)
    for c in range(attn_width // LANES):
        lanes = slice(c * LANES, (c + 1) * LANES)
        q_ref[0, :, lanes] = _head_norm_rope(y_ref[:, lanes], qg, cos, sin).astype(BF16)
        kr = _head_norm_rope(y_ref[:, attn_width + c * LANES:attn_width + (c + 1) * LANES], kg_ref[...], cos, sin)
        k_ref[0, :, lanes] = kr.astype(BF16)
        for r in range(tm // MOBA_BLOCK):
            km = jnp.sum(kr[r * MOBA_BLOCK:(r + 1) * MOBA_BLOCK], axis=0, keepdims=True) * (1.0 / MOBA_BLOCK)
            kmean_ref[0, r, :, lanes] = km

    u = jnp.dot(hb, w_ref[:, 0:pool_width], preferred_element_type=F32)
    end = POOL_HALO + tm
    ubuf_ref[POOL_HALO:end, :] = u
    gd = pool_width // len(POOL_WINDOWS)
    src = ubuf_ref
    for k in range(1, len(POOL_WINDOWS) + 1):
        dst = lva_ref if k % 2 else lvb_ref
        assert POOL_WINDOWS[k - 1] == 2 ** k
        shift, r0, l0 = 2 ** (k - 1), SUBLANES * k, (k - 1) * gd
        dst[r0:end, l0:] = src[r0:end, l0:] + src[r0 - shift:end - shift, l0:]
        src = dst
    t = s * tm + lax.broadcasted_iota(jnp.int32, (tm, 1), 0)
    for g, w in enumerate(POOL_WINDOWS):
        lanes = slice(g * gd, (g + 1) * gd)
        win = (lva_ref if (g + 1) % 2 else lvb_ref)[POOL_HALO:end, lanes]
        cnt = jnp.minimum(t + 1, w).astype(F32)
        pooled = win / cnt - u[:, lanes]
        mixed = jnp.dot(pooled.astype(BF16), pw_ref[g], preferred_element_type=F32)
        pool_ref[0, :, lanes] = (mixed * ps_ref[:, lanes]).astype(BF16)
    ubuf_ref[0:POOL_HALO, :] = ubuf_ref[tm:end, :]

    v = jnp.dot(hb, w_ref[:, v0:v0 + attn_width], preferred_element_type=F32)
    lane = lax.broadcasted_iota(jnp.int32, (1, LANES), 1)
    for pr in range(attn_width // LANES):
        pair = v[:, pr * LANES:(pr + 1) * LANES]
        for hh in range(2):
            keep = (lane < HEAD_DIM) if hh == 0 else (lane >= HEAD_DIM)
            ones = jnp.where(lane == _ones_lane(hh), 1.0, 0.0)
            h = 2 * pr + hh
            v_ref[0, :, h * LANES:(h + 1) * LANES] = jnp.where(keep, pair, ones).astype(BF16)


def _ones_lane(hh):
    return HEAD_DIM if hh == 0 else 0


def _in_proj(x, g, w_in, pool_w, pool_scale, qg, kg, cos, sin, *, tm):
    B, S, D = x.shape
    n_groups, gd, _ = pool_w.shape
    pool_width = n_groups * gd
    attn_width = (w_in.shape[1] - pool_width) // 3
    nblk = S // MOBA_BLOCK
    kern = functools.partial(_in_proj_kernel, tm=tm, pool_width=pool_width, attn_width=attn_width)
    const2 = lambda b, s: (0, 0)
    tile3 = lambda b, s: (b, s, 0)
    return pl.pallas_call(
        kern,
        grid=(B, S // tm),
        in_specs=[
            pl.BlockSpec((1, tm, D), tile3),
            pl.BlockSpec((1, D), const2),
            pl.BlockSpec(w_in.shape, const2),
            pl.BlockSpec(pool_w.shape, lambda b, s: (0, 0, 0)),
            pl.BlockSpec((1, pool_width), const2),
            pl.BlockSpec((1, LANES), const2),
            pl.BlockSpec((1, LANES), const2),
            pl.BlockSpec((tm, LANES), lambda b, s: (s, 0)),
            pl.BlockSpec((tm, LANES), lambda b, s: (s, 0)),
        ],
        out_specs=[
            pl.BlockSpec((1, tm, pool_width), tile3),
            pl.BlockSpec((1, tm, attn_width), tile3),
            pl.BlockSpec((1, tm, attn_width), tile3),
            pl.BlockSpec((1, tm, 2 * attn_width), tile3),
            pl.BlockSpec((1, tm // MOBA_BLOCK, 1, attn_width), lambda b, s: (b, s, 0, 0)),
        ],
        out_shape=[
            jax.ShapeDtypeStruct((B, S, pool_width), BF16),
            jax.ShapeDtypeStruct((B, S, attn_width), BF16),
            jax.ShapeDtypeStruct((B, S, attn_width), BF16),
            jax.ShapeDtypeStruct((B, S, 2 * attn_width), BF16),
            jax.ShapeDtypeStruct((B, nblk, 1, attn_width), F32),
        ],
        scratch_shapes=[pltpu.VMEM((POOL_HALO + tm, pool_width), F32)] * 3 + [pltpu.VMEM((tm, 2 * attn_width), F32)],
        compiler_params=pltpu.CompilerParams(
            dimension_semantics=("arbitrary", "arbitrary"), vmem_limit_bytes=VMEM_LIMIT),
        name="in_proj",
    )(x, g, w_in, pool_w, pool_scale, qg, kg, cos, sin)


def _moba_kernel(q_ref, k_ref, v_ref, kmean_ref, o_ref, qh_ref, sel_ref, s_ref, m_ref, acc_ref, *, nblk, npairs):
    i = pl.program_id(2)
    blk = MOBA_BLOCK
    first = _first_head_lanes()
    bidx = lax.broadcasted_iota(jnp.int32, (nblk, 1), 0)
    eligible = bidx < i

    for pr in range(npairs):
        lanes = slice(pr * LANES, (pr + 1) * LANES)
        qs = q_ref[0, :, lanes]
        zero = jnp.zeros_like(qs)
        km = kmean_ref[0, :, lanes]
        km_hi = km.astype(BF16)
        km_split = jnp.concatenate([km_hi, (km - km_hi.astype(F32)).astype(BF16)], axis=0)
        for hh in range(2):
            h = 2 * pr + hh
            qm = jnp.where(first, qs, zero) if hh == 0 else jnp.where(first, zero, qs)
            qh_ref[h] = qm.T
            both = jnp.dot(km_split, qh_ref[h], preferred_element_type=F32)
            gate = jnp.where(eligible, both[:nblk] + both[nblk:], -jnp.inf)
            picked = jnp.zeros((nblk, blk), jnp.bool_)
            for _ in range(MOBA_TOPK):
                best = jnp.max(gate, axis=0, keepdims=True)
                where_best = jnp.min(jnp.where(gate == best, bidx, nblk), axis=0, keepdims=True)
                hit = bidx == where_best
                picked = picked | hit
                gate = jnp.where(hit, -jnp.inf, gate)
            sel = jnp.where(picked & eligible, 1.0, 0.0)
            for j in range(nblk):
                sel_ref[h, j] = sel[j:j + 1, :]

    m_ref[...] = jnp.full(m_ref.shape, NEG, F32)
    acc_ref[...] = jnp.zeros_like(acc_ref)

    def scores(slot, kblock):
        off = pl.multiple_of(kblock * blk, blk)
        for pr in range(npairs):
            kb = k_ref[0, pl.ds(off, blk), pr * LANES:(pr + 1) * LANES]
            for hh in range(2):
                h = 2 * pr + hh
                s_ref[slot, h] = jnp.dot(kb, qh_ref[h], preferred_element_type=F32)

    def accumulate(slot, kblock, past):
        off = pl.multiple_of(kblock * blk, blk)
        for pr in range(npairs):
            for hh in range(2):
                h = 2 * pr + hh
                vb = v_ref[0, pl.ds(off, blk), h * LANES:(h + 1) * LANES]
                st = s_ref[slot, h]
                m = m_ref[h]
                if not past:
                    causal = (lax.broadcasted_iota(jnp.int32, (blk, blk), 0)
                              <= lax.broadcasted_iota(jnp.int32, (blk, blk), 1))
                    st = jnp.where(causal, st, NEG)
                    m_new = jnp.maximum(m, jnp.max(st, axis=0, keepdims=True))
                    p = jnp.exp2(st - m_new)
                else:
                    picked = sel_ref[h, kblock] > 0.0
                    m_new = jnp.maximum(m, jnp.where(picked, jnp.max(st, axis=0, keepdims=True), NEG))
                    p = jnp.exp2(st - jnp.where(picked, m_new, -NEG))
                alpha = jnp.exp2(m - m_new)
                m_ref[h] = m_new
                pv = lax.dot_general(vb, p.astype(BF16), TN_DIMS, preferred_element_type=F32)
                ones_row = _ones_lane(hh)
                acc_ref[h, 0:HEAD_DIM] = alpha * acc_ref[h, 0:HEAD_DIM] + pv[hh * HEAD_DIM:(hh + 1) * HEAD_DIM]
                acc_ref[h, HEAD_DIM:] = alpha * acc_ref[h, HEAD_DIM:] + pv[ones_row:ones_row + SUBLANES]

    scores(0, 0)

    def pair(t, _):
        a = 2 * t
        scores(1, a + 1)
        accumulate(0, a, True)
        scores(0, a + 2)
        accumulate(1, a + 1, True)
        return 0

    lax.fori_loop(0, i // 2, pair, 0)

    @pl.when(i % 2 == 0)
    def _():
        accumulate(0, i, False)

    @pl.when(i % 2 == 1)
    def _():
        scores(1, i)
        accumulate(0, i - 1, True)
        accumulate(1, i, False)

    for pr in range(npairs):
        halves = [acc_ref[h, 0:HEAD_DIM] / acc_ref[h, HEAD_DIM:HEAD_DIM + 1] for h in (2 * pr, 2 * pr + 1)]
        o_ref[0, :, pr * LANES:(pr + 1) * LANES] = jnp.concatenate(halves, axis=0).T.astype(BF16)


def _moba(q, k, v, kmean, *, npairs):
    B, S, W = q.shape
    nblk = S // MOBA_BLOCK
    wstep = npairs * LANES
    kern = functools.partial(_moba_kernel, nblk=nblk, npairs=npairs)
    return pl.pallas_call(
        kern,
        grid=(B, W // wstep, nblk),
        in_specs=[
            pl.BlockSpec((1, MOBA_BLOCK, wstep), lambda b, g, i: (b, i, g)),
            pl.BlockSpec((1, S, wstep), lambda b, g, i: (b, 0, g)),
            pl.BlockSpec((1, S, 2 * wstep), lambda b, g, i: (b, 0, g)),
            pl.BlockSpec((1, nblk, wstep), lambda b, g, i: (b, 0, g)),
        ],
        out_specs=pl.BlockSpec((1, MOBA_BLOCK, wstep), lambda b, g, i: (b, i, g)),
        out_shape=jax.ShapeDtypeStruct((B, S, W), BF16),
        scratch_shapes=[pltpu.VMEM((2 * npairs, LANES, MOBA_BLOCK), BF16),
                        pltpu.VMEM((2 * npairs, nblk, 1, MOBA_BLOCK), F32),
                        pltpu.VMEM((2, 2 * npairs, MOBA_BLOCK, MOBA_BLOCK), F32),
                        pltpu.VMEM((2 * npairs, 1, MOBA_BLOCK), F32),
                        pltpu.VMEM((2 * npairs, HEAD_DIM + SUBLANES, MOBA_BLOCK), F32)],
        compiler_params=pltpu.CompilerParams(
            dimension_semantics=("arbitrary", "arbitrary", "arbitrary"), vmem_limit_bytes=VMEM_LIMIT),
        name="moba",
    )(q, k, v, kmean)


def _mem_kv_kernel(mem_ref, g_ref, w_ref, kg_ref, k_ref, v_ref, *, d_model):
    mb = _rms(mem_ref[0], g_ref[...]).astype(BF16)
    hd = d_model // XATTN_HEADS
    for h in range(XATTN_HEADS):
        lanes = slice(h * hd, (h + 1) * hd)
        kh = jnp.dot(mb, w_ref[:, lanes], preferred_element_type=F32)
        k_ref[0, :, lanes] = _rms(kh, kg_ref[...]).astype(BF16)
    v_ref[0] = jnp.dot(mb, w_ref[:, d_model:2 * d_model], preferred_element_type=F32).astype(BF16)


def _mem_kv(mem, g, w_xkv, kg):
    B, M, D = mem.shape
    const2 = lambda b: (0, 0)
    blk3 = lambda b: (b, 0, 0)
    return pl.pallas_call(
        functools.partial(_mem_kv_kernel, d_model=D),
        grid=(B,),
        in_specs=[pl.BlockSpec((1, M, D), blk3), pl.BlockSpec((1, D), const2),
                  pl.BlockSpec(w_xkv.shape, const2), pl.BlockSpec((1, D // XATTN_HEADS), const2)],
        out_specs=[pl.BlockSpec((1, M, D), blk3), pl.BlockSpec((1, M, D), blk3)],
        out_shape=[jax.ShapeDtypeStruct((B, M, D), BF16)] * 2,
        compiler_params=pltpu.CompilerParams(
            dimension_semantics=("arbitrary",), vmem_limit_bytes=VMEM_LIMIT),
        name="mem_kv",
    )(mem, g, w_xkv, kg)


def _mix_xattn_kernel(x_ref, pool_ref, attn_ref, wo_ref, g_ref, wq_ref, qg_ref, k_ref, v_ref, wxo_ref,
                      o_ref, q_ref, qn_ref, s_ref, p_ref, oh_ref, *, pool_width, d_model):
    x1 = (x_ref[0]
          + jnp.dot(pool_ref[0], wo_ref[0:pool_width, :], preferred_element_type=F32)
          + jnp.dot(attn_ref[0], wo_ref[pool_width:, :], preferred_element_type=F32))
    hb = _rms(x1, g_ref[...]).astype(BF16)
    hd = d_model // XATTN_HEADS
    heads = [slice(h * hd, (h + 1) * hd) for h in range(XATTN_HEADS)]
    q_ref[...] = jnp.dot(hb, wq_ref[...], preferred_element_type=F32)
    for lanes in heads:
        qn_ref[:, lanes] = (_rms(q_ref[:, lanes], qg_ref[...]) * (hd ** -0.5)).astype(BF16)
    for h, lanes in enumerate(heads):
        s_ref[h] = lax.dot_general(qn_ref[:, lanes], k_ref[0, :, lanes], NT_DIMS,
                                   preferred_element_type=F32)
    for h in range(XATTN_HEADS):
        s = s_ref[h]
        p = jnp.exp(s - jnp.max(s, axis=-1, keepdims=True))
        p_ref[h] = (p / jnp.sum(p, axis=-1, keepdims=True)).astype(BF16)
    for h, lanes in enumerate(heads):
        oh_ref[:, lanes] = jnp.dot(p_ref[h], v_ref[0, :, lanes], preferred_element_type=F32).astype(BF16)
    o_ref[0] = x1 + jnp.dot(oh_ref[...], wxo_ref[...], preferred_element_type=F32)


def _mix_xattn(x, pool, attn, w_out, g, w_xq, qg, kx, vx, w_xo, *, tm):
    B, S, D = x.shape
    M = kx.shape[1]
    pool_width = pool.shape[2]
    const2 = lambda b, s: (0, 0)
    tile3 = lambda b, s: (b, s, 0)
    mem3 = lambda b, s: (b, 0, 0)
    return pl.pallas_call(
        functools.partial(_mix_xattn_kernel, pool_width=pool_width, d_model=D),
        grid=(B, S // tm),
        in_specs=[
            pl.BlockSpec((1, tm, D), tile3),
            pl.BlockSpec((1, tm, pool_width), tile3),
            pl.BlockSpec((1, tm, attn.shape[2]), tile3),
            pl.BlockSpec(w_out.shape, const2),
            pl.BlockSpec((1, D), const2),
            pl.BlockSpec(w_xq.shape, const2),
            pl.BlockSpec((1, D // XATTN_HEADS), const2),
            pl.BlockSpec((1, M, D), mem3),
            pl.BlockSpec((1, M, D), mem3),
            pl.BlockSpec(w_xo.shape, const2),
        ],
        out_specs=pl.BlockSpec((1, tm, D), tile3),
        out_shape=jax.ShapeDtypeStruct((B, S, D), F32),
        scratch_shapes=[pltpu.VMEM((tm, D), F32), pltpu.VMEM((tm, D), BF16),
                        pltpu.VMEM((XATTN_HEADS, tm, M), F32), pltpu.VMEM((XATTN_HEADS, tm, M), BF16),
                        pltpu.VMEM((tm, D), BF16)],
        compiler_params=pltpu.CompilerParams(
            dimension_semantics=("arbitrary", "arbitrary"), vmem_limit_bytes=VMEM_LIMIT),
        name="mix_xattn",
    )(x, pool, attn, w_out, g, w_xq, qg, kx, vx, w_xo)


def _conv_ffn_kernel(x_ref, g_ref, wu_ref, cw_ref, cb_ref, wd_ref, o_ref,
                     hb_ref, halo_ref, buf_ref, act_ref, *, tm, n_chunks):
    s = pl.program_id(1)

    @pl.when(s == 0)
    def _():
        halo_ref[...] = jnp.zeros_like(halo_ref)

    x = x_ref[0]
    hb_ref[...] = _rms(x, g_ref[...]).astype(BF16)

    def conv(c, slot):
        cols = slice(c * FF_CHUNK, (c + 1) * FF_CHUNK)
        up = jnp.dot(hb_ref[...], wu_ref[:, cols], preferred_element_type=F32)
        buf_ref[slot, 0:CONV_HALO, :] = halo_ref[c]
        buf_ref[slot, CONV_HALO:CONV_HALO + tm, :] = up
        halo_ref[c] = up[tm - CONV_HALO:, :]
        cw = cw_ref[:, cols]
        y = up * cw[CONV_WIDTH - 1:CONV_WIDTH, :] + cb_ref[:, cols]
        for d in range(1, CONV_WIDTH):
            y = y + buf_ref[slot, CONV_HALO - d:CONV_HALO - d + tm, :] * cw[CONV_WIDTH - 1 - d:CONV_WIDTH - d, :]
        return y

    for c in range(n_chunks):
        gate = conv(c, 2 * (c % 2))
        val = conv(c + n_chunks, 2 * (c % 2) + 1)
        act_ref[:, c * FF_CHUNK:(c + 1) * FF_CHUNK] = (gate * (1.0 / (1.0 + jnp.exp2(gate * -LOG2E))) * val).astype(BF16)
    o_ref[0] = x + jnp.dot(act_ref[...], wd_ref[...], preferred_element_type=F32)


def _conv_ffn(x, g, w_up, conv_w, conv_b, w_down, *, tm):
    B, S, D = x.shape
    d_ff = w_down.shape[0]
    n_chunks = d_ff // FF_CHUNK
    const2 = lambda b, s: (0, 0)
    tile3 = lambda b, s: (b, s, 0)
    return pl.pallas_call(
        functools.partial(_conv_ffn_kernel, tm=tm, n_chunks=n_chunks),
        grid=(B, S // tm),
        in_specs=[
            pl.BlockSpec((1, tm, D), tile3),
            pl.BlockSpec((1, D), const2),
            pl.BlockSpec(w_up.shape, const2),
            pl.BlockSpec(conv_w.shape, const2),
            pl.BlockSpec(conv_b.shape, const2),
            pl.BlockSpec(w_down.shape, const2),
        ],
        out_specs=pl.BlockSpec((1, tm, D), tile3),
        out_shape=jax.ShapeDtypeStruct((B, S, D), F32),
        scratch_shapes=[
            pltpu.VMEM((tm, D), BF16),
            pltpu.VMEM((2 * n_chunks, CONV_HALO, FF_CHUNK), F32),
            pltpu.VMEM((4, CONV_HALO + tm, FF_CHUNK), F32),
            pltpu.VMEM((tm, d_ff), BF16),
        ],
        compiler_params=pltpu.CompilerParams(
            dimension_semantics=("arbitrary", "arbitrary"), vmem_limit_bytes=VMEM_LIMIT),
        name="conv_ffn",
    )(x, g, w_up, conv_w, conv_b, w_down)


def _rope_tables(S):
    half = HEAD_DIM // 2
    inv_freq = ROPE_THETA ** (-jnp.arange(half, dtype=F32) / half)
    ang = jnp.arange(S).astype(F32)[:, None] * inv_freq[None, :]
    cos, sin = jnp.cos(ang), jnp.sin(ang)
    return (jnp.concatenate([cos, cos, cos, cos], axis=-1),
            jnp.concatenate([-sin, -sin, sin, sin], axis=-1))


def _slab_gain(g):
    lo, hi = g[:HEAD_DIM // 2], g[HEAD_DIM // 2:]
    return jnp.concatenate([lo, lo, hi, hi]).reshape(1, LANES)


def _slab_columns(w):
    rows, width = w.shape
    w = w.reshape(rows, width // LANES, 2, 2, HEAD_DIM // 2)
    return w.transpose(0, 1, 3, 2, 4).reshape(rows, width)


def kernel(x, mem, norm_mix_g, w_in, pool_w, pool_scale, q_norm_g, k_norm_g, w_out, norm_xattn_g, norm_mem_g, w_xq, w_xkv, xq_norm_g, xk_norm_g, w_xo, norm_ffn_g, w_up, conv_w, conv_b, w_down):
    B, S, D = x.shape
    depth = w_in.shape[0]
    d_ff = w_down.shape[1]
    assert S % 512 == 0 and d_ff % FF_CHUNK == 0
    cos, sin = _rope_tables(S)
    row = lambda a: a.reshape(1, -1)
    pool_width = pool_w.shape[1] * pool_w.shape[2]
    attn_width = (w_in.shape[2] - pool_width) // 3
    q0, k0, v0 = pool_width, pool_width + attn_width, pool_width + 2 * attn_width
    for l in range(depth):
        w_in_l = jnp.concatenate([w_in[l][:, :q0], _slab_columns(w_in[l][:, q0:k0]),
                                  _slab_columns(w_in[l][:, k0:v0]), w_in[l][:, v0:]], axis=1).astype(BF16)
        pool, q, k, v, kmean = _in_proj(
            x, row(norm_mix_g[l]), w_in_l, pool_w[l].astype(BF16), row(pool_scale[l]),
            _slab_gain(q_norm_g[l]), _slab_gain(k_norm_g[l]), cos, sin, tm=512)
        attn = _moba(q, k, v, kmean.reshape(B, S // MOBA_BLOCK, -1), npairs=4)
        kx, vx = _mem_kv(mem, row(norm_mem_g[l]), w_xkv[l].astype(BF16), row(xk_norm_g[l]))
        x = _mix_xattn(x, pool, attn, w_out[l].astype(BF16), row(norm_xattn_g[l]), w_xq[l].astype(BF16),
                       row(xq_norm_g[l]), kx, vx, w_xo[l].astype(BF16), tm=512)
        x = _conv_ffn(x, row(norm_ffn_g[l]), w_up[l].astype(BF16), conv_w[l], row(conv_b[l]), w_down[l].astype(BF16),
                      tm=512)
    return x
```

```python
import functools

import jax
import jax.numpy as jnp
from jax import lax
from jax.experimental import pallas as pl
from jax.experimental.pallas import tpu as pltpu

F32 = jnp.float32
BF16 = jnp.bfloat16

EPS = 1e-6
LANES = 128
SUBLANES = 8
MXU_COLS = 256
HEAD_DIM = 64
POOL_WINDOWS = (2, 4, 8, 16)
POOL_HALO = SUBLANES * len(POOL_WINDOWS)
MOBA_BLOCK = 256
MOBA_TOPK = 3
ROPE_THETA = 10000.0
XATTN_HEADS = 4
CONV_WIDTH = 3
CONV_HALO = 8
FF_CHUNK = 256
NEG = -1e30
LOG2E = 1.4426950408889634
VMEM_LIMIT = 56 * 1024 * 1024

NT_DIMS = (((1,), (1,)), ((), ()))
TN_DIMS = (((0,), (0,)), ((), ()))


def _rms(x, g):
    return x * lax.rsqrt(jnp.mean(x * x, axis=-1, keepdims=True) + EPS) * g


def _first_head_lanes():
    lane = lax.broadcasted_iota(jnp.int32, (1, LANES), 1)
    return (lane & (HEAD_DIM // 2)) == 0


def _head_norm_rope(y, g, cos, sin_signed):
    first = _first_head_lanes()
    sq = y * y
    s0 = jnp.sum(jnp.where(first, sq, 0.0), axis=-1, keepdims=True)
    s1 = jnp.sum(jnp.where(first, 0.0, sq), axis=-1, keepdims=True)
    r = lax.rsqrt(jnp.where(first, s0, s1) * (1.0 / HEAD_DIM) + EPS)
    yn = y * r * g
    return yn * cos + pltpu.roll(yn, LANES // 2, axis=1) * sin_signed


def _in_proj_kernel(x_ref, g_ref, w_ref, pw_ref, ps_ref, qg_ref, kg_ref, cos_ref, sin_ref,
                    pool_ref, q_ref, k_ref, v_ref, kmean_ref, ubuf_ref, lva_ref, lvb_ref, y_ref,
                    *, tm, pool_width, attn_width):
    s = pl.program_id(1)

    @pl.when(s == 0)
    def _():
        ubuf_ref[0:POOL_HALO, :] = jnp.zeros((POOL_HALO, pool_width), F32)

    hb = _rms(x_ref[0], g_ref[...]).astype(BF16)
    q0, v0 = pool_width, pool_width + 2 * attn_width

    y_ref[...] = jnp.dot(hb, w_ref[:, q0:v0], preferred_element_type=F32)
    cos = cos_ref[...]
    sin = sin_ref[...]
    qg = qg_ref[...] * (HEAD_DIM ** -0.5 * LOG2E)
    for c in range(attn_width // LANES):
        lanes = slice(c * LANES, (c + 1) * LANES)
        q_ref[0, :, lanes] = _head_norm_rope(y_ref[:, lanes], qg, cos, sin).astype(BF16)
        kr = _head_norm_rope(y_ref[:, attn_width + c * LANES:attn_width + (c + 1) * LANES], kg_ref[...], cos, sin)
        k_ref[0, :, lanes] = kr.astype(BF16)
        for r in range(tm // MOBA_BLOCK):
            km = jnp.sum(kr[r * MOBA_BLOCK:(r + 1) * MOBA_BLOCK], axis=0, keepdims=True) * (1.0 / MOBA_BLOCK)
            kmean_ref[0, r, :, lanes] = km

    u = jnp.dot(hb, w_ref[:, 0:pool_width], preferred_element_type=F32)
    end = POOL_HALO + tm
    ubuf_ref[POOL_HALO:end, :] = u
    gd = pool_width // len(POOL_WINDOWS)
    src = ubuf_ref
    for k in range(1, len(POOL_WINDOWS) + 1):
        dst = lva_ref if k % 2 else lvb_ref
        assert POOL_WINDOWS[k - 1] == 2 ** k
        shift, r0, l0 = 2 ** (k - 1), SUBLANES * k, (k - 1) * gd
        dst[r0:end, l0:] = src[r0:end, l0:] + src[r0 - shift:end - shift, l0:]
        src = dst
    t = s * tm + lax.broadcasted_iota(jnp.int32, (tm, 1), 0)
    for g, w in enumerate(POOL_WINDOWS):
        lanes = slice(g * gd, (g + 1) * gd)
        win = (lva_ref if (g + 1) % 2 else lvb_ref)[POOL_HALO:end, lanes]
        cnt = jnp.minimum(t + 1, w).astype(F32)
        pooled = win / cnt - u[:, lanes]
        mixed = jnp.dot(pooled.astype(BF16), pw_ref[g], preferred_element_type=F32)
        pool_ref[0, :, lanes] = (mixed * ps_ref[:, lanes]).astype(BF16)
    ubuf_ref[0:POOL_HALO, :] = ubuf_ref[tm:end, :]

    v = jnp.dot(hb, w_ref[:, v0:v0 + attn_width], preferred_element_type=F32)
    lane = lax.broadcasted_iota(jnp.int32, (1, LANES), 1)
    for pr in range(attn_width // LANES):
        pair = v[:, pr * LANES:(pr + 1) * LANES]
        for hh in range(2):
            keep = (lane < HEAD_DIM) if hh == 0 else (lane >= HEAD_DIM)
            ones = jnp.where(lane == _ones_lane(hh), 1.0, 0.0)
            h = 2 * pr + hh
            v_ref[0, :, h * LANES:(h + 1) * LANES] = jnp.where(keep, pair, ones).astype(BF16)


def _ones_lane(hh):
    return HEAD_DIM if hh == 0 else 0


def _in_proj(x, g, w_in, pool_w, pool_scale, qg, kg, cos, sin, *, tm):
    B, S, D = x.shape
    n_groups, gd, _ = pool_w.shape
    pool_width = n_groups * gd
    attn_width = (w_in.shape[1] - pool_width) // 3
    nblk = S // MOBA_BLOCK
    kern = functools.partial(_in_proj_kernel, tm=tm, pool_width=pool_width, attn_width=attn_width)
    const2 = lambda b, s: (0, 0)
    tile3 = lambda b, s: (b, s, 0)
    return pl.pallas_call(
        kern,
        grid=(B, S // tm),
        in_specs=[
            pl.BlockSpec((1, tm, D), tile3),
            pl.BlockSpec((1, D), const2),
            pl.BlockSpec(w_in.shape, const2),
            pl.BlockSpec(pool_w.shape, lambda b, s: (0, 0, 0)),
            pl.BlockSpec((1, pool_width), const2),
            pl.BlockSpec((1, LANES), const2),
            pl.BlockSpec((1, LANES), const2),
            pl.BlockSpec((tm, LANES), lambda b, s: (s, 0)),
            pl.BlockSpec((tm, LANES), lambda b, s: (s, 0)),
        ],
        out_specs=[
            pl.BlockSpec((1, tm, pool_width), tile3),
            pl.BlockSpec((1, tm, attn_width), tile3),
            pl.BlockSpec((1, tm, attn_width), tile3),
            pl.BlockSpec((1, tm, 2 * attn_width), tile3),
            pl.BlockSpec((1, tm // MOBA_BLOCK, 1, attn_width), lambda b, s: (b, s, 0, 0)),
        ],
        out_shape=[
            jax.ShapeDtypeStruct((B, S, pool_width), BF16),
            jax.ShapeDtypeStruct((B, S, attn_width), BF16),
            jax.ShapeDtypeStruct((B, S, attn_width), BF16),
            jax.ShapeDtypeStruct((B, S, 2 * attn_width), BF16),
            jax.ShapeDtypeStruct((B, nblk, 1, attn_width), F32),
        ],
        scratch_shapes=[pltpu.VMEM((POOL_HALO + tm, pool_width), F32)] * 3 + [pltpu.VMEM((tm, 2 * attn_width), F32)],
        compiler_params=pltpu.CompilerParams(
            dimension_semantics=("arbitrary", "arbitrary"), vmem_limit_bytes=VMEM_LIMIT),
        name="in_proj",
    )(x, g, w_in, pool_w, pool_scale, qg, kg, cos, sin)


def _moba_kernel(q_ref, k_ref, v_ref, kmean_ref, o_ref, qh_ref, sel_ref, s_ref, m_ref, acc_ref, *, nblk, npairs):
    i = pl.program_id(2)
    blk = MOBA_BLOCK
    first = _first_head_lanes()
    bidx = lax.broadcasted_iota(jnp.int32, (nblk, 1), 0)
    eligible = bidx < i

    for pr in range(npairs):
        lanes = slice(pr * LANES, (pr + 1) * LANES)
        qs = q_ref[0, :, lanes]
        zero = jnp.zeros_like(qs)
        km = kmean_ref[0, :, lanes]
        km_hi = km.astype(BF16)
        km_split = jnp.concatenate([km_hi, (km - km_hi.astype(F32)).astype(BF16)], axis=0)
        for hh in range(2):
            h = 2 * pr + hh
            qm = jnp.where(first, qs, zero) if hh == 0 else jnp.where(first, zero, qs)
            qh_ref[h] = qm.T
            both = jnp.dot(km_split, qh_ref[h], preferred_element_type=F32)
            gate = jnp.where(eligible, both[:nblk] + both[nblk:], -jnp.inf)
            picked = jnp.zeros((nblk, blk), jnp.bool_)
            for _ in range(MOBA_TOPK):
                best = jnp.max(gate, axis=0, keepdims=True)
                where_best = jnp.min(jnp.where(gate == best, bidx, nblk), axis=0, keepdims=True)
                hit = bidx == where_best
                picked = picked | hit
                gate = jnp.where(hit, -jnp.inf, gate)
            sel = jnp.where(picked & eligible, 1.0, 0.0)
            for j in range(nblk):
                sel_ref[h, j] = sel[j:j + 1, :]

    m_ref[...] = jnp.full(m_ref.shape, NEG, F32)
    acc_ref[...] = jnp.zeros_like(acc_ref)

    def scores(slot, kblock):
        off = pl.multiple_of(kblock * blk, blk)
        for pr in range(npairs):
            kb = k_ref[0, pl.ds(off, blk), pr * LANES:(pr + 1) * LANES]
            for hh in range(2):
                h = 2 * pr + hh
                s_ref[slot, h] = jnp.dot(kb, qh_ref[h], preferred_element_type=F32)

    def accumulate(slot, kblock, past):
        off = pl.multiple_of(kblock * blk, blk)
        for pr in range(npairs):
            for hh in range(2):
                h = 2 * pr + hh
                vb = v_ref[0, pl.ds(off, blk), h * LANES:(h + 1) * LANES]
                st = s_ref[slot, h]
                m = m_ref[h]
                if not past:
                    causal = (lax.broadcasted_iota(jnp.int32, (blk, blk), 0)
                              <= lax.broadcasted_iota(jnp.int32, (blk, blk), 1))
                    st = jnp.where(causal, st, NEG)
                    m_new = jnp.maximum(m, jnp.max(st, axis=0, keepdims=True))
                    p = jnp.exp2(st - m_new)
                else:
                    picked = sel_ref[h, kblock] > 0.0
                    m_new = jnp.maximum(m, jnp.where(picked, jnp.max(st, axis=0, keepdims=True), NEG))
                    p = jnp.exp2(st - jnp.where(picked, m_new, -NEG))
                alpha = jnp.exp2(m - m_new)
                m_ref[h] = m_new
                pv = lax.dot_general(vb, p.astype(BF16), TN_DIMS, preferred_element_type=F32)
                ones_row = _ones_lane(hh)
                acc_ref[h, 0:HEAD_DIM] = alpha * acc_ref[h, 0:HEAD_DIM] + pv[hh * HEAD_DIM:(hh + 1) * HEAD_DIM]
                acc_ref[h, HEAD_DIM:] = alpha * acc_ref[h, HEAD_DIM:] + pv[ones_row:ones_row + SUBLANES]

    scores(0, 0)

    def pair(t, _):
        a = 2 * t
        scores(1, a + 1)
        accumulate(0, a, True)
        scores(0, a + 2)
        accumulate(1, a + 1, True)
        return 0

    lax.fori_loop(0, i // 2, pair, 0)

    @pl.when(i % 2 == 0)
    def _():
        accumulate(0, i, False)

    @pl.when(i % 2 == 1)
    def _():
        scores(1, i)
        accumulate(0, i - 1, True)
        accumulate(1, i, False)

    for pr in range(npairs):
        halves = [acc_ref[h, 0:HEAD_DIM] / acc_ref[h, HEAD_DIM:HEAD_DIM + 1] for h in (2 * pr, 2 * pr + 1)]
        o_ref[0, :, pr * LANES:(pr + 1) * LANES] = jnp.concatenate(halves, axis=0).T.astype(BF16)


def _moba(q, k, v, kmean, *, npairs):
    B, S, W = q.shape
    nblk = S // MOBA_BLOCK
    wstep = npairs * LANES
    kern = functools.partial(_moba_kernel, nblk=nblk, npairs=npairs)
    return pl.pallas_call(
        kern,
        grid=(B, W // wstep, nblk),
        in_specs=[
            pl.BlockSpec((1, MOBA_BLOCK, wstep), lambda b, g, i: (b, i, g)),
            pl.BlockSpec((1, S, wstep), lambda b, g, i: (b, 0, g)),
            pl.BlockSpec((1, S, 2 * wstep), lambda b, g, i: (b, 0, g)),
            pl.BlockSpec((1, nblk, wstep), lambda b, g, i: (b, 0, g)),
        ],
        out_specs=pl.BlockSpec((1, MOBA_BLOCK, wstep), lambda b, g, i: (b, i, g)),
        out_shape=jax.ShapeDtypeStruct((B, S, W), BF16),
        scratch_shapes=[pltpu.VMEM((2 * npairs, LANES, MOBA_BLOCK), BF16),
                        pltpu.VMEM((2 * npairs, nblk, 1, MOBA_BLOCK), F32),
                        pltpu.VMEM((2, 2 * npairs, MOBA_BLOCK, MOBA_BLOCK), F32),
                        pltpu.VMEM((2 * npairs, 1, MOBA_BLOCK), F32),
                        pltpu.VMEM((2 * npairs, HEAD_DIM + SUBLANES, MOBA_BLOCK), F32)],
        compiler_params=pltpu.CompilerParams(
            dimension_semantics=("arbitrary", "arbitrary", "arbitrary"), vmem_limit_bytes=VMEM_LIMIT),
        name="moba",
    )(q, k, v, kmean)


def _mem_kv_kernel(mem_ref, g_ref, w_ref, kg_ref, k_ref, v_ref, *, d_model):
    mb = _rms(mem_ref[0], g_ref[...]).astype(BF16)
    hd = d_model // XATTN_HEADS
    for h in range(XATTN_HEADS):
        lanes = slice(h * hd, (h + 1) * hd)
        kh = jnp.dot(mb, w_ref[:, lanes], preferred_element_type=F32)
        k_ref[0, :, lanes] = _rms(kh, kg_ref[...]).astype(BF16)
    v_ref[0] = jnp.dot(mb, w_ref[:, d_model:2 * d_model], preferred_element_type=F32).astype(BF16)


def _mem_kv(mem, g, w_xkv, kg):
    B, M, D = mem.shape
    const2 = lambda b: (0, 0)
    blk3 = lambda b: (b, 0, 0)
    return pl.pallas_call(
        functools.partial(_mem_kv_kernel, d_model=D),
        grid=(B,),
        in_specs=[pl.BlockSpec((1, M, D), blk3), pl.BlockSpec((1, D), const2),
                  pl.BlockSpec(w_xkv.shape, const2), pl.BlockSpec((1, D // XATTN_HEADS), const2)],
        out_specs=[pl.BlockSpec((1, M, D), blk3), pl.BlockSpec((1, M, D), blk3)],
        out_shape=[jax.ShapeDtypeStruct((B, M, D), BF16)] * 2,
        compiler_params=pltpu.CompilerParams(
            dimension_semantics=("arbitrary",), vmem_limit_bytes=VMEM_LIMIT),
        name="mem_kv",
    )(mem, g, w_xkv, kg)


def _mix_xattn_kernel(x_ref, pool_ref, attn_ref, wo_ref, g_ref, wq_ref, qg_ref, k_ref, v_ref, wxo_ref,
                      o_ref, q_ref, qn_ref, s_ref, p_ref, oh_ref, *, pool_width, d_model):
    x1 = (x_ref[0]
          + jnp.dot(pool_ref[0], wo_ref[0:pool_width, :], preferred_element_type=F32)
          + jnp.dot(attn_ref[0], wo_ref[pool_width:, :], preferred_element_type=F32))
    hb = _rms(x1, g_ref[...]).astype(BF16)
    hd = d_model // XATTN_HEADS
    heads = [slice(h * hd, (h + 1) * hd) for h in range(XATTN_HEADS)]
    q_ref[...] = jnp.dot(hb, wq_ref[...], preferred_element_type=F32)
    for lanes in heads:
        qn_ref[:, lanes] = (_rms(q_ref[:, lanes], qg_ref[...]) * (hd ** -0.5)).astype(BF16)
    for h, lanes in enumerate(heads):
        s_ref[h] = lax.dot_general(qn_ref[:, lanes], k_ref[0, :, lanes], NT_DIMS,
                                   preferred_element_type=F32)
    for h in range(XATTN_HEADS):
        s = s_ref[h]
        p = jnp.exp(s - jnp.max(s, axis=-1, keepdims=True))
        p_ref[h] = (p / jnp.sum(p, axis=-1, keepdims=True)).astype(BF16)
    for h, lanes in enumerate(heads):
        oh_ref[:, lanes] = jnp.dot(p_ref[h], v_ref[0, :, lanes], preferred_element_type=F32).astype(BF16)
    o_ref[0] = x1 + jnp.dot(oh_ref[...], wxo_ref[...], preferred_element_type=F32)


def _mix_xattn(x, pool, attn, w_out, g, w_xq, qg, kx, vx, w_xo, *, tm):
    B, S, D = x.shape
    M = kx.shape[1]
    pool_width = pool.shape[2]
    const2 = lambda b, s: (0, 0)
    tile3 = lambda b, s: (b, s, 0)
    mem3 = lambda b, s: (b, 0, 0)
    return pl.pallas_call(
        functools.partial(_mix_xattn_kernel, pool_width=pool_width, d_model=D),
        grid=(B, S // tm),
        in_specs=[
            pl.BlockSpec((1, tm, D), tile3),
            pl.BlockSpec((1, tm, pool_width), tile3),
            pl.BlockSpec((1, tm, attn.shape[2]), tile3),
            pl.BlockSpec(w_out.shape, const2),
            pl.BlockSpec((1, D), const2),
            pl.BlockSpec(w_xq.shape, const2),
            pl.BlockSpec((1, D // XATTN_HEADS), const2),
            pl.BlockSpec((1, M, D), mem3),
            pl.BlockSpec((1, M, D), mem3),
            pl.BlockSpec(w_xo.shape, const2),
        ],
        out_specs=pl.BlockSpec((1, tm, D), tile3),
        out_shape=jax.ShapeDtypeStruct((B, S, D), F32),
        scratch_shapes=[pltpu.VMEM((tm, D), F32), pltpu.VMEM((tm, D), BF16),
                        pltpu.VMEM((XATTN_HEADS, tm, M), F32), pltpu.VMEM((XATTN_HEADS, tm, M), BF16),
                        pltpu.VMEM((tm, D), BF16)],
        compiler_params=pltpu.CompilerParams(
            dimension_semantics=("arbitrary", "arbitrary"), vmem_limit_bytes=VMEM_LIMIT),
        name="mix_xattn",
    )(x, pool, attn, w_out, g, w_xq, qg, kx, vx, w_xo)


def _conv_ffn_kernel(x_ref, g_ref, wu_ref, cw_ref, cb_ref, wd_ref, o_ref,
                     hb_ref, halo_ref, buf_ref, act_ref, *, tm, n_chunks):
    s = pl.program_id(1)

    @pl.when(s == 0)
    def _():
        halo_ref[...] = jnp.zeros_like(halo_ref)

    x = x_ref[0]
    hb_ref[...] = _rms(x, g_ref[...]).astype(BF16)

    def conv(c, slot):
        cols = slice(c * FF_CHUNK, (c + 1) * FF_CHUNK)
        up = jnp.dot(hb_ref[...], wu_ref[:, cols], preferred_element_type=F32)
        buf_ref[slot, 0:CONV_HALO, :] = halo_ref[c]
        buf_ref[slot, CONV_HALO:CONV_HALO + tm, :] = up
        halo_ref[c] = up[tm - CONV_HALO:, :]
        cw = cw_ref[:, cols]
        y = up * cw[CONV_WIDTH - 1:CONV_WIDTH, :] + cb_ref[:, cols]
        for d in range(1, CONV_WIDTH):
            y = y + buf_ref[slot, CONV_HALO - d:CONV_HALO - d + tm, :] * cw[CONV_WIDTH - 1 - d:CONV_WIDTH - d, :]
        return y

    for c in range(n_chunks):
        gate = conv(c, 2 * (c % 2))
        val = conv(c + n_chunks, 2 * (c % 2) + 1)
        act_ref[:, c * FF_CHUNK:(c + 1) * FF_CHUNK] = (gate * (1.0 / (1.0 + jnp.exp2(gate * -LOG2E))) * val).astype(BF16)
    o_ref[0] = x + jnp.dot(act_ref[...], wd_ref[...], preferred_element_type=F32)


def _conv_ffn(x, g, w_up, conv_w, conv_b, w_down, *, tm):
    B, S, D = x.shape
    d_ff = w_down.shape[0]
    n_chunks = d_ff // FF_CHUNK
    const2 = lambda b, s: (0, 0)
    tile3 = lambda b, s: (b, s, 0)
    return pl.pallas_call(
        functools.partial(_conv_ffn_kernel, tm=tm, n_chunks=n_chunks),
        grid=(B, S // tm),
        in_specs=[
            pl.BlockSpec((1, tm, D), tile3),
            pl.BlockSpec((1, D), const2),
            pl.BlockSpec(w_up.shape, const2),
            pl.BlockSpec(conv_w.shape, const2),
            pl.BlockSpec(conv_b.shape, const2),
            pl.BlockSpec(w_down.shape, const2),
        ],
        out_specs=pl.BlockSpec((1, tm, D), tile3),
        out_shape=jax.ShapeDtypeStruct((B, S, D), F32),
        scratch_shapes=[
            pltpu.VMEM((tm, D), BF16),
            pltpu.VMEM((2 * n_chunks, CONV_HALO, FF_CHUNK), F32),
            pltpu.VMEM((4, CONV_HALO + tm, FF_CHUNK), F32),
            pltpu.VMEM((tm, d_ff), BF16),
        ],
        compiler_params=pltpu.CompilerParams(
            dimension_semantics=("arbitrary", "arbitrary"), vmem_limit_bytes=VMEM_LIMIT),
        name="conv_ffn",
    )(x, g, w_up, conv_w, conv_b, w_down)


def _rope_tables(S):
    half = HEAD_DIM // 2
    inv_freq = ROPE_THETA ** (-jnp.arange(half, dtype=F32) / half)
    ang = jnp.arange(S).astype(F32)[:, None] * inv_freq[None, :]
    cos, sin = jnp.cos(ang), jnp.sin(ang)
    return (jnp.concatenate([cos, cos, cos, cos], axis=-1),
            jnp.concatenate([-sin, -sin, sin, sin], axis=-1))


def _slab_gain(g):
    lo, hi = g[:HEAD_DIM // 2], g[HEAD_DIM // 2:]
    return jnp.concatenate([lo, lo, hi, hi]).reshape(1, LANES)


def _slab_columns(w):
    rows, width = w.shape
    w = w.reshape(rows, width // LANES, 2, 2, HEAD_DIM // 2)
    return w.transpose(0, 1, 3, 2, 4).reshape(rows, width)


def kernel(x, mem, norm_mix_g, w_in, pool_w, pool_scale, q_norm_g, k_norm_g, w_out, norm_xattn_g, norm_mem_g, w_xq, w_xkv, xq_norm_g, xk_norm_g, w_xo, norm_ffn_g, w_up, conv_w, conv_b, w_down):
    B, S, D = x.shape
    depth = w_in.shape[0]
    d_ff = w_down.shape[1]
    assert S % 512 == 0 and d_ff % FF_CHUNK == 0
    cos, sin = _rope_tables(S)
    row = lambda a: a.reshape(1, -1)
    pool_width = pool_w.shape[1] * pool_w.shape[2]
    attn_width = (w_in.shape[2] - pool_width) // 3
    q0, k0, v0 = pool_width, pool_width + attn_width, pool_width + 2 * attn_width
    for l in range(depth):
        w_in_l = jnp.concatenate([w_in[l][:, :q0], _slab_columns(w_in[l][:, q0:k0]),
                                  _slab_columns(w_in[l][:, k0:v0]), w_in[l][:, v0:]], axis=1).astype(BF16)
        pool, q, k, v, kmean = _in_proj(
            x, row(norm_mix_g[l]), w_in_l, pool_w[l].astype(BF16), row(pool_scale[l]),
            _slab_gain(q_norm_g[l]), _slab_gain(k_norm_g[l]), cos, sin, tm=512)
        attn = _moba(q, k, v, kmean.reshape(B, S // MOBA_BLOCK, -1), npairs=4)
        kx, vx = _mem_kv(mem, row(norm_mem_g[l]), w_xkv[l].astype(BF16), row(xk_norm_g[l]))
        x = _mix_xattn(x, pool, attn, w_out[l].astype(BF16), row(norm_xattn_g[l]), w_xq[l].astype(BF16),
                       row(xq_norm_g[l]), kx, vx, w_xo[l].astype(BF16), tm=1024)
        x = _conv_ffn(x, row(norm_ffn_g[l]), w_up[l].astype(BF16), conv_w[l], row(conv_b[l]), w_down[l].astype(BF16),
                      tm=512)
    return x
```

```python
import functools

import jax
import jax.numpy as jnp
from jax import lax
from jax.experimental import pallas as pl
from jax.experimental.pallas import tpu as pltpu

F32 = jnp.float32
BF16 = jnp.bfloat16

EPS = 1e-6
LANES = 128
SUBLANES = 8
HEAD_DIM = 64
POOL_WINDOWS = (2, 4, 8, 16)
POOL_HALO = SUBLANES * len(POOL_WINDOWS)
MOBA_BLOCK = 256
MOBA_TOPK = 3
ROPE_THETA = 10000.0
XATTN_HEADS = 4
CONV_WIDTH = 3
CONV_HALO = 8
FF_CHUNK = 256
NEG = -1e30
LOG2E = 1.4426950408889634
VMEM_LIMIT = 56 * 1024 * 1024
IN_PROJ_TILE = 512
MIX_TILE = 1024
FFN_TILE = 512

NT_DIMS = (((1,), (1,)), ((), ()))
TN_DIMS = (((0,), (0,)), ((), ()))


def _rms(x, g):
    return x * lax.rsqrt(jnp.mean(x * x, axis=-1, keepdims=True) + EPS) * g


def _first_head_lanes():
    lane = lax.broadcasted_iota(jnp.int32, (1, LANES), 1)
    return (lane & (HEAD_DIM // 2)) == 0


def _head_norm_rope(y, g, cos, sin_signed):
    first = _first_head_lanes()
    sq = y * y
    s0 = jnp.sum(jnp.where(first, sq, 0.0), axis=-1, keepdims=True)
    s1 = jnp.sum(jnp.where(first, 0.0, sq), axis=-1, keepdims=True)
    r = lax.rsqrt(jnp.where(first, s0, s1) * (1.0 / HEAD_DIM) + EPS)
    yn = y * r * g
    return yn * cos + pltpu.roll(yn, LANES // 2, axis=1) * sin_signed


def _in_proj_kernel(x_ref, g_ref, w_ref, pw_ref, ps_ref, qg_ref, kg_ref, cos_ref, sin_ref,
                    pool_ref, q_ref, k_ref, v_ref, kmean_ref, ubuf_ref, lva_ref, lvb_ref, y_ref,
                    *, tm, pool_width, attn_width):
    s = pl.program_id(1)

    @pl.when(s == 0)
    def _():
        ubuf_ref[0:POOL_HALO, :] = jnp.zeros((POOL_HALO, pool_width), F32)

    hb = _rms(x_ref[0], g_ref[...]).astype(BF16)
    q0, v0 = pool_width, pool_width + 2 * attn_width

    y_ref[...] = jnp.dot(hb, w_ref[:, q0:v0], preferred_element_type=F32)
    cos = cos_ref[...]
    sin = sin_ref[...]
    qg = qg_ref[...] * (HEAD_DIM ** -0.5 * LOG2E)
    for c in range(attn_width // LANES):
        lanes = slice(c * LANES, (c + 1) * LANES)
        q_ref[0, :, lanes] = _head_norm_rope(y_ref[:, lanes], qg, cos, sin).astype(BF16)
        kr = _head_norm_rope(y_ref[:, attn_width + c * LANES:attn_width + (c + 1) * LANES], kg_ref[...], cos, sin)
        k_ref[0, :, lanes] = kr.astype(BF16)
        for r in range(tm // MOBA_BLOCK):
            km = jnp.sum(kr[r * MOBA_BLOCK:(r + 1) * MOBA_BLOCK], axis=0, keepdims=True) * (1.0 / MOBA_BLOCK)
            kmean_ref[0, r, :, lanes] = km

    u = jnp.dot(hb, w_ref[:, 0:pool_width], preferred_element_type=F32)
    end = POOL_HALO + tm
    ubuf_ref[POOL_HALO:end, :] = u
    gd = pool_width // len(POOL_WINDOWS)
    src = ubuf_ref
    for k in range(1, len(POOL_WINDOWS) + 1):
        dst = lva_ref if k % 2 else lvb_ref
        assert POOL_WINDOWS[k - 1] == 2 ** k
        shift, r0, l0 = 2 ** (k - 1), SUBLANES * k, (k - 1) * gd
        dst[r0:end, l0:] = src[r0:end, l0:] + src[r0 - shift:end - shift, l0:]
        src = dst
    t = s * tm + lax.broadcasted_iota(jnp.int32, (tm, 1), 0)
    for g, w in enumerate(POOL_WINDOWS):
        lanes = slice(g * gd, (g + 1) * gd)
        win = (lva_ref if (g + 1) % 2 else lvb_ref)[POOL_HALO:end, lanes]
        cnt = jnp.minimum(t + 1, w).astype(F32)
        pooled = win / cnt - u[:, lanes]
        mixed = jnp.dot(pooled.astype(BF16), pw_ref[g], preferred_element_type=F32)
        pool_ref[0, :, lanes] = (mixed * ps_ref[:, lanes]).astype(BF16)
    ubuf_ref[0:POOL_HALO, :] = ubuf_ref[tm:end, :]

    v = jnp.dot(hb, w_ref[:, v0:v0 + attn_width], preferred_element_type=F32)
    lane = lax.broadcasted_iota(jnp.int32, (1, LANES), 1)
    for pr in range(attn_width // LANES):
        pair = v[:, pr * LANES:(pr + 1) * LANES]
        for hh in range(2):
            keep = (lane < HEAD_DIM) if hh == 0 else (lane >= HEAD_DIM)
            ones = jnp.where(lane == _ones_lane(hh), 1.0, 0.0)
            h = 2 * pr + hh
            v_ref[0, :, h * LANES:(h + 1) * LANES] = jnp.where(keep, pair, ones).astype(BF16)


def _ones_lane(hh):
    return HEAD_DIM if hh == 0 else 0


def _in_proj(x, g, w_in, pool_w, pool_scale, qg, kg, cos, sin, *, tm):
    B, S, D = x.shape
    n_groups, gd, _ = pool_w.shape
    pool_width = n_groups * gd
    attn_width = (w_in.shape[1] - pool_width) // 3
    nblk = S // MOBA_BLOCK
    kern = functools.partial(_in_proj_kernel, tm=tm, pool_width=pool_width, attn_width=attn_width)
    const2 = lambda b, s: (0, 0)
    tile3 = lambda b, s: (b, s, 0)
    return pl.pallas_call(
        kern,
        grid=(B, S // tm),
        in_specs=[
            pl.BlockSpec((1, tm, D), tile3),
            pl.BlockSpec((1, D), const2),
            pl.BlockSpec(w_in.shape, const2),
            pl.BlockSpec(pool_w.shape, lambda b, s: (0, 0, 0)),
            pl.BlockSpec((1, pool_width), const2),
            pl.BlockSpec((1, LANES), const2),
            pl.BlockSpec((1, LANES), const2),
            pl.BlockSpec((tm, LANES), lambda b, s: (s, 0)),
            pl.BlockSpec((tm, LANES), lambda b, s: (s, 0)),
        ],
        out_specs=[
            pl.BlockSpec((1, tm, pool_width), tile3),
            pl.BlockSpec((1, tm, attn_width), tile3),
            pl.BlockSpec((1, tm, attn_width), tile3),
            pl.BlockSpec((1, tm, 2 * attn_width), tile3),
            pl.BlockSpec((1, tm // MOBA_BLOCK, 1, attn_width), lambda b, s: (b, s, 0, 0)),
        ],
        out_shape=[
            jax.ShapeDtypeStruct((B, S, pool_width), BF16),
            jax.ShapeDtypeStruct((B, S, attn_width), BF16),
            jax.ShapeDtypeStruct((B, S, attn_width), BF16),
            jax.ShapeDtypeStruct((B, S, 2 * attn_width), BF16),
            jax.ShapeDtypeStruct((B, nblk, 1, attn_width), F32),
        ],
        scratch_shapes=[pltpu.VMEM((POOL_HALO + tm, pool_width), F32)] * 3 + [pltpu.VMEM((tm, 2 * attn_width), F32)],
        compiler_params=pltpu.CompilerParams(
            dimension_semantics=("arbitrary", "arbitrary"), vmem_limit_bytes=VMEM_LIMIT),
        name="in_proj",
    )(x, g, w_in, pool_w, pool_scale, qg, kg, cos, sin)


def _moba_kernel(q_ref, k_ref, v_ref, kmean_ref, o_ref, qh_ref, sel_ref, s_ref, m_ref, acc_ref, *, nblk, npairs):
    i = pl.program_id(2)
    blk = MOBA_BLOCK
    first = _first_head_lanes()
    bidx = lax.broadcasted_iota(jnp.int32, (nblk, 1), 0)
    eligible = bidx < i

    for pr in range(npairs):
        lanes = slice(pr * LANES, (pr + 1) * LANES)
        qs = q_ref[0, :, lanes]
        zero = jnp.zeros_like(qs)
        km = kmean_ref[0, :, lanes]
        km_hi = km.astype(BF16)
        km_split = jnp.concatenate([km_hi, (km - km_hi.astype(F32)).astype(BF16)], axis=0)
        for hh in range(2):
            h = 2 * pr + hh
            qm = jnp.where(first, qs, zero) if hh == 0 else jnp.where(first, zero, qs)
            qh_ref[h] = qm.T
            both = jnp.dot(km_split, qh_ref[h], preferred_element_type=F32)
            gate = jnp.where(eligible, both[:nblk] + both[nblk:], -jnp.inf)
            picked = jnp.zeros((nblk, blk), jnp.bool_)
            for _ in range(MOBA_TOPK):
                best = jnp.max(gate, axis=0, keepdims=True)
                where_best = jnp.min(jnp.where(gate == best, bidx, nblk), axis=0, keepdims=True)
                hit = bidx == where_best
                picked = picked | hit
                gate = jnp.where(hit, -jnp.inf, gate)
            sel = jnp.where(picked & eligible, 1.0, 0.0)
            for j in range(nblk):
                sel_ref[h, j] = sel[j:j + 1, :]

    m_ref[...] = jnp.full(m_ref.shape, NEG, F32)
    acc_ref[...] = jnp.zeros_like(acc_ref)

    def scores(slot, kblock):
        off = pl.multiple_of(kblock * blk, blk)
        for pr in range(npairs):
            kb = k_ref[0, pl.ds(off, blk), pr * LANES:(pr + 1) * LANES]
            for hh in range(2):
                h = 2 * pr + hh
                s_ref[slot, h] = jnp.dot(kb, qh_ref[h], preferred_element_type=F32)

    def accumulate(slot, kblock, past):
        off = pl.multiple_of(kblock * blk, blk)
        for pr in range(npairs):
            for hh in range(2):
                h = 2 * pr + hh
                vb = v_ref[0, pl.ds(off, blk), h * LANES:(h + 1) * LANES]
                st = s_ref[slot, h]
                m = m_ref[h]
                if not past:
                    causal = (lax.broadcasted_iota(jnp.int32, (blk, blk), 0)
                              <= lax.broadcasted_iota(jnp.int32, (blk, blk), 1))
                    st = jnp.where(causal, st, NEG)
                    m_new = jnp.maximum(m, jnp.max(st, axis=0, keepdims=True))
                    p = jnp.exp2(st - m_new)
                else:
                    picked = sel_ref[h, kblock] > 0.0
                    m_new = jnp.maximum(m, jnp.where(picked, jnp.max(st, axis=0, keepdims=True), NEG))
                    p = jnp.exp2(st - jnp.where(picked, m_new, -NEG))
                alpha = jnp.exp2(m - m_new)
                m_ref[h] = m_new
                pv = lax.dot_general(vb, p.astype(BF16), TN_DIMS, preferred_element_type=F32)
                ones_row = _ones_lane(hh)
                acc_ref[h, 0:HEAD_DIM] = alpha * acc_ref[h, 0:HEAD_DIM] + pv[hh * HEAD_DIM:(hh + 1) * HEAD_DIM]
                acc_ref[h, HEAD_DIM:] = alpha * acc_ref[h, HEAD_DIM:] + pv[ones_row:ones_row + SUBLANES]

    scores(0, 0)

    def pair(t, _):
        a = 2 * t
        scores(1, a + 1)
        accumulate(0, a, True)
        scores(0, a + 2)
        accumulate(1, a + 1, True)
        return 0

    lax.fori_loop(0, i // 2, pair, 0)

    @pl.when(i % 2 == 0)
    def _():
        accumulate(0, i, False)

    @pl.when(i % 2 == 1)
    def _():
        scores(1, i)
        accumulate(0, i - 1, True)
        accumulate(1, i, False)

    for pr in range(npairs):
        halves = [acc_ref[h, 0:HEAD_DIM] / acc_ref[h, HEAD_DIM:HEAD_DIM + 1] for h in (2 * pr, 2 * pr + 1)]
        o_ref[0, :, pr * LANES:(pr + 1) * LANES] = jnp.concatenate(halves, axis=0).T.astype(BF16)


def _moba(q, k, v, kmean, *, npairs):
    B, S, W = q.shape
    nblk = S // MOBA_BLOCK
    wstep = npairs * LANES
    kern = functools.partial(_moba_kernel, nblk=nblk, npairs=npairs)
    return pl.pallas_call(
        kern,
        grid=(B, W // wstep, nblk),
        in_specs=[
            pl.BlockSpec((1, MOBA_BLOCK, wstep), lambda b, g, i: (b, i, g)),
            pl.BlockSpec((1, S, wstep), lambda b, g, i: (b, 0, g)),
            pl.BlockSpec((1, S, 2 * wstep), lambda b, g, i: (b, 0, g)),
            pl.BlockSpec((1, nblk, wstep), lambda b, g, i: (b, 0, g)),
        ],
        out_specs=pl.BlockSpec((1, MOBA_BLOCK, wstep), lambda b, g, i: (b, i, g)),
        out_shape=jax.ShapeDtypeStruct((B, S, W), BF16),
        scratch_shapes=[pltpu.VMEM((2 * npairs, LANES, MOBA_BLOCK), BF16),
                        pltpu.VMEM((2 * npairs, nblk, 1, MOBA_BLOCK), F32),
                        pltpu.VMEM((2, 2 * npairs, MOBA_BLOCK, MOBA_BLOCK), F32),
                        pltpu.VMEM((2 * npairs, 1, MOBA_BLOCK), F32),
                        pltpu.VMEM((2 * npairs, HEAD_DIM + SUBLANES, MOBA_BLOCK), F32)],
        compiler_params=pltpu.CompilerParams(
            dimension_semantics=("arbitrary", "arbitrary", "arbitrary"), vmem_limit_bytes=VMEM_LIMIT),
        name="moba",
    )(q, k, v, kmean)


def _mem_kv_kernel(mem_ref, g_ref, w_ref, kg_ref, k_ref, v_ref, *, d_model):
    mb = _rms(mem_ref[0], g_ref[...]).astype(BF16)
    hd = d_model // XATTN_HEADS
    for h in range(XATTN_HEADS):
        lanes = slice(h * hd, (h + 1) * hd)
        kh = jnp.dot(mb, w_ref[:, lanes], preferred_element_type=F32)
        k_ref[0, :, lanes] = _rms(kh, kg_ref[...]).astype(BF16)
    v_ref[0] = jnp.dot(mb, w_ref[:, d_model:2 * d_model], preferred_element_type=F32).astype(BF16)


def _mem_kv(mem, g, w_xkv, kg):
    B, M, D = mem.shape
    const2 = lambda b: (0, 0)
    blk3 = lambda b: (b, 0, 0)
    return pl.pallas_call(
        functools.partial(_mem_kv_kernel, d_model=D),
        grid=(B,),
        in_specs=[pl.BlockSpec((1, M, D), blk3), pl.BlockSpec((1, D), const2),
                  pl.BlockSpec(w_xkv.shape, const2), pl.BlockSpec((1, D // XATTN_HEADS), const2)],
        out_specs=[pl.BlockSpec((1, M, D), blk3), pl.BlockSpec((1, M, D), blk3)],
        out_shape=[jax.ShapeDtypeStruct((B, M, D), BF16)] * 2,
        compiler_params=pltpu.CompilerParams(
            dimension_semantics=("arbitrary",), vmem_limit_bytes=VMEM_LIMIT),
        name="mem_kv",
    )(mem, g, w_xkv, kg)


def _mix_xattn_kernel(x_ref, pool_ref, attn_ref, wo_ref, g_ref, wq_ref, qg_ref, k_ref, v_ref, wxo_ref,
                      o_ref, q_ref, qn_ref, s_ref, p_ref, oh_ref, *, pool_width, d_model):
    x1 = (x_ref[0]
          + jnp.dot(pool_ref[0], wo_ref[0:pool_width, :], preferred_element_type=F32)
          + jnp.dot(attn_ref[0], wo_ref[pool_width:, :], preferred_element_type=F32))
    hb = _rms(x1, g_ref[...]).astype(BF16)
    hd = d_model // XATTN_HEADS
    heads = [slice(h * hd, (h + 1) * hd) for h in range(XATTN_HEADS)]
    q_ref[...] = jnp.dot(hb, wq_ref[...], preferred_element_type=F32)
    for lanes in heads:
        qn_ref[:, lanes] = (_rms(q_ref[:, lanes], qg_ref[...]) * (hd ** -0.5)).astype(BF16)
    for h, lanes in enumerate(heads):
        s_ref[h] = lax.dot_general(qn_ref[:, lanes], k_ref[0, :, lanes], NT_DIMS,
                                   preferred_element_type=F32)
    for h in range(XATTN_HEADS):
        s = s_ref[h]
        p = jnp.exp(s - jnp.max(s, axis=-1, keepdims=True))
        p_ref[h] = (p / jnp.sum(p, axis=-1, keepdims=True)).astype(BF16)
    for h, lanes in enumerate(heads):
        oh_ref[:, lanes] = jnp.dot(p_ref[h], v_ref[0, :, lanes], preferred_element_type=F32).astype(BF16)
    o_ref[0] = x1 + jnp.dot(oh_ref[...], wxo_ref[...], preferred_element_type=F32)


def _mix_xattn(x, pool, attn, w_out, g, w_xq, qg, kx, vx, w_xo, *, tm):
    B, S, D = x.shape
    M = kx.shape[1]
    pool_width = pool.shape[2]
    const2 = lambda b, s: (0, 0)
    tile3 = lambda b, s: (b, s, 0)
    mem3 = lambda b, s: (b, 0, 0)
    return pl.pallas_call(
        functools.partial(_mix_xattn_kernel, pool_width=pool_width, d_model=D),
        grid=(B, S // tm),
        in_specs=[
            pl.BlockSpec((1, tm, D), tile3),
            pl.BlockSpec((1, tm, pool_width), tile3),
            pl.BlockSpec((1, tm, attn.shape[2]), tile3),
            pl.BlockSpec(w_out.shape, const2),
            pl.BlockSpec((1, D), const2),
            pl.BlockSpec(w_xq.shape, const2),
            pl.BlockSpec((1, D // XATTN_HEADS), const2),
            pl.BlockSpec((1, M, D), mem3),
            pl.BlockSpec((1, M, D), mem3),
            pl.BlockSpec(w_xo.shape, const2),
        ],
        out_specs=pl.BlockSpec((1, tm, D), tile3),
        out_shape=jax.ShapeDtypeStruct((B, S, D), F32),
        scratch_shapes=[pltpu.VMEM((tm, D), F32), pltpu.VMEM((tm, D), BF16),
                        pltpu.VMEM((XATTN_HEADS, tm, M), F32), pltpu.VMEM((XATTN_HEADS, tm, M), BF16),
                        pltpu.VMEM((tm, D), BF16)],
        compiler_params=pltpu.CompilerParams(
            dimension_semantics=("arbitrary", "arbitrary"), vmem_limit_bytes=VMEM_LIMIT),
        name="mix_xattn",
    )(x, pool, attn, w_out, g, w_xq, qg, kx, vx, w_xo)


def _conv_ffn_kernel(x_ref, g_ref, wu_ref, cw_ref, cb_ref, wd_ref, o_ref,
                     hb_ref, halo_ref, buf_ref, act_ref, *, tm, n_chunks):
    s = pl.program_id(1)

    @pl.when(s == 0)
    def _():
        halo_ref[...] = jnp.zeros_like(halo_ref)

    x = x_ref[0]
    hb_ref[...] = _rms(x, g_ref[...]).astype(BF16)

    def conv(c, slot):
        cols = slice(c * FF_CHUNK, (c + 1) * FF_CHUNK)
        up = jnp.dot(hb_ref[...], wu_ref[:, cols], preferred_element_type=F32)
        buf_ref[slot, 0:CONV_HALO, :] = halo_ref[c]
        buf_ref[slot, CONV_HALO:CONV_HALO + tm, :] = up
        halo_ref[c] = up[tm - CONV_HALO:, :]
        cw = cw_ref[:, cols]
        y = up * cw[CONV_WIDTH - 1:CONV_WIDTH, :] + cb_ref[:, cols]
        for d in range(1, CONV_WIDTH):
            y = y + buf_ref[slot, CONV_HALO - d:CONV_HALO - d + tm, :] * cw[CONV_WIDTH - 1 - d:CONV_WIDTH - d, :]
        return y

    for c in range(n_chunks):
        gate = conv(c, 2 * (c % 2))
        val = conv(c + n_chunks, 2 * (c % 2) + 1)
        act_ref[:, c * FF_CHUNK:(c + 1) * FF_CHUNK] = (gate * (1.0 / (1.0 + jnp.exp2(gate * -LOG2E))) * val).astype(BF16)
    o_ref[0] = x + jnp.dot(act_ref[...], wd_ref[...], preferred_element_type=F32)


def _conv_ffn(x, g, w_up, conv_w, conv_b, w_down, *, tm):
    B, S, D = x.shape
    d_ff = w_down.shape[0]
    n_chunks = d_ff // FF_CHUNK
    const2 = lambda b, s: (0, 0)
    tile3 = lambda b, s: (b, s, 0)
    return pl.pallas_call(
        functools.partial(_conv_ffn_kernel, tm=tm, n_chunks=n_chunks),
        grid=(B, S // tm),
        in_specs=[
            pl.BlockSpec((1, tm, D), tile3),
            pl.BlockSpec((1, D), const2),
            pl.BlockSpec(w_up.shape, const2),
            pl.BlockSpec(conv_w.shape, const2),
            pl.BlockSpec(conv_b.shape, const2),
            pl.BlockSpec(w_down.shape, const2),
        ],
        out_specs=pl.BlockSpec((1, tm, D), tile3),
        out_shape=jax.ShapeDtypeStruct((B, S, D), F32),
        scratch_shapes=[
            pltpu.VMEM((tm, D), BF16),
            pltpu.VMEM((2 * n_chunks, CONV_HALO, FF_CHUNK), F32),
            pltpu.VMEM((4, CONV_HALO + tm, FF_CHUNK), F32),
            pltpu.VMEM((tm, d_ff), BF16),
        ],
        compiler_params=pltpu.CompilerParams(
            dimension_semantics=("arbitrary", "arbitrary"), vmem_limit_bytes=VMEM_LIMIT),
        name="conv_ffn",
    )(x, g, w_up, conv_w, conv_b, w_down)


def _rope_tables(S):
    half = HEAD_DIM // 2
    inv_freq = ROPE_THETA ** (-jnp.arange(half, dtype=F32) / half)
    ang = jnp.arange(S).astype(F32)[:, None] * inv_freq[None, :]
    cos, sin = jnp.cos(ang), jnp.sin(ang)
    return (jnp.concatenate([cos, cos, cos, cos], axis=-1),
            jnp.concatenate([-sin, -sin, sin, sin], axis=-1))


def _slab_gain(g):
    lo, hi = g[:HEAD_DIM // 2], g[HEAD_DIM // 2:]
    return jnp.concatenate([lo, lo, hi, hi]).reshape(1, LANES)


def _slab_columns(w):
    rows, width = w.shape
    w = w.reshape(rows, width // LANES, 2, 2, HEAD_DIM // 2)
    return w.transpose(0, 1, 3, 2, 4).reshape(rows, width)


def kernel(x, mem, norm_mix_g, w_in, pool_w, pool_scale, q_norm_g, k_norm_g, w_out, norm_xattn_g, norm_mem_g, w_xq, w_xkv, xq_norm_g, xk_norm_g, w_xo, norm_ffn_g, w_up, conv_w, conv_b, w_down):
    B, S, D = x.shape
    depth = w_in.shape[0]
    d_ff = w_down.shape[1]
    assert all(S % t == 0 for t in (IN_PROJ_TILE, MIX_TILE, FFN_TILE, MOBA_BLOCK)) and d_ff % FF_CHUNK == 0
    cos, sin = _rope_tables(S)
    row = lambda a: a.reshape(1, -1)
    pool_width = pool_w.shape[1] * pool_w.shape[2]
    attn_width = (w_in.shape[2] - pool_width) // 3
    q0, k0, v0 = pool_width, pool_width + attn_width, pool_width + 2 * attn_width
    for l in range(depth):
        w_in_l = jnp.concatenate([w_in[l][:, :q0], _slab_columns(w_in[l][:, q0:k0]),
                                  _slab_columns(w_in[l][:, k0:v0]), w_in[l][:, v0:]], axis=1).astype(BF16)
        pool, q, k, v, kmean = _in_proj(
            x, row(norm_mix_g[l]), w_in_l, pool_w[l].astype(BF16), row(pool_scale[l]),
            _slab_gain(q_norm_g[l]), _slab_gain(k_norm_g[l]), cos, sin, tm=IN_PROJ_TILE)
        attn = _moba(q, k, v, kmean.reshape(B, S // MOBA_BLOCK, -1), npairs=attn_width // LANES)
        kx, vx = _mem_kv(mem, row(norm_mem_g[l]), w_xkv[l].astype(BF16), row(xk_norm_g[l]))
        x = _mix_xattn(x, pool, attn, w_out[l].astype(BF16), row(norm_xattn_g[l]), w_xq[l].astype(BF16),
                       row(xq_norm_g[l]), kx, vx, w_xo[l].astype(BF16), tm=MIX_TILE)
        x = _conv_ffn(x, row(norm_ffn_g[l]), w_up[l].astype(BF16), conv_w[l], row(conv_b[l]), w_down[l].astype(BF16),
                      tm=FFN_TILE)
    return x
```

```python
import functools

import jax
import jax.numpy as jnp
from jax import lax
from jax.experimental import pallas as pl
from jax.experimental.pallas import tpu as pltpu

F32 = jnp.float32
BF16 = jnp.bfloat16

EPS = 1e-6
LANES = 128
SUBLANES = 8
HEAD_DIM = 64
POOL_WINDOWS = (2, 4, 8, 16)
POOL_HALO = SUBLANES * len(POOL_WINDOWS)
MOBA_BLOCK = 256
MOBA_TOPK = 3
ROPE_THETA = 10000.0
XATTN_HEADS = 4
CONV_WIDTH = 3
CONV_HALO = 8
FF_CHUNK = 256
NEG = -1e30
LOG2E = 1.4426950408889634
VMEM_LIMIT = 56 * 1024 * 1024
IN_PROJ_TILE = 512
MIX_TILE = 1024
FFN_TILE = 512

NT_DIMS = (((1,), (1,)), ((), ()))
TN_DIMS = (((0,), (0,)), ((), ()))


def _rms(x, g):
    return x * lax.rsqrt(jnp.mean(x * x, axis=-1, keepdims=True) + EPS) * g


def _first_head_lanes():
    lane = lax.broadcasted_iota(jnp.int32, (1, LANES), 1)
    return (lane & (HEAD_DIM // 2)) == 0


def _head_norm_rope(y, g, cos, sin_signed):
    first = _first_head_lanes()
    sq = y * y
    s0 = jnp.sum(jnp.where(first, sq, 0.0), axis=-1, keepdims=True)
    s1 = jnp.sum(jnp.where(first, 0.0, sq), axis=-1, keepdims=True)
    r = lax.rsqrt(jnp.where(first, s0, s1) * (1.0 / HEAD_DIM) + EPS)
    yn = y * r * g
    return yn * cos + pltpu.roll(yn, LANES // 2, axis=1) * sin_signed


def _in_proj_kernel(x_ref, g_ref, w_ref, pw_ref, ps_ref, qg_ref, kg_ref, cos_ref, sin_ref,
                    pool_ref, q_ref, k_ref, v_ref, kmean_ref, ubuf_ref, lva_ref, lvb_ref, y_ref,
                    *, tm, pool_width, attn_width):
    s = pl.program_id(1)

    @pl.when(s == 0)
    def _():
        ubuf_ref[0:POOL_HALO, :] = jnp.zeros((POOL_HALO, pool_width), F32)

    hb = _rms(x_ref[0], g_ref[...]).astype(BF16)
    q0, v0 = pool_width, pool_width + 2 * attn_width

    y_ref[...] = jnp.dot(hb, w_ref[:, q0:v0], preferred_element_type=F32)
    cos = cos_ref[...]
    sin = sin_ref[...]
    qg = qg_ref[...] * (HEAD_DIM ** -0.5 * LOG2E)
    for c in range(attn_width // LANES):
        lanes = slice(c * LANES, (c + 1) * LANES)
        q_ref[0, :, lanes] = _head_norm_rope(y_ref[:, lanes], qg, cos, sin).astype(BF16)
        kr = _head_norm_rope(y_ref[:, attn_width + c * LANES:attn_width + (c + 1) * LANES], kg_ref[...], cos, sin)
        k_ref[0, :, lanes] = kr.astype(BF16)
        for r in range(tm // MOBA_BLOCK):
            km = jnp.sum(kr[r * MOBA_BLOCK:(r + 1) * MOBA_BLOCK], axis=0, keepdims=True) * (1.0 / MOBA_BLOCK)
            kmean_ref[0, r, :, lanes] = km

    u = jnp.dot(hb, w_ref[:, 0:pool_width], preferred_element_type=F32)
    end = POOL_HALO + tm
    ubuf_ref[POOL_HALO:end, :] = u
    gd = pool_width // len(POOL_WINDOWS)
    src = ubuf_ref
    for k in range(1, len(POOL_WINDOWS) + 1):
        dst = lva_ref if k % 2 else lvb_ref
        assert POOL_WINDOWS[k - 1] == 2 ** k
        shift, r0, l0 = 2 ** (k - 1), SUBLANES * k, (k - 1) * gd
        dst[r0:end, l0:] = src[r0:end, l0:] + src[r0 - shift:end - shift, l0:]
        src = dst
    t = s * tm + lax.broadcasted_iota(jnp.int32, (tm, 1), 0)
    for g, w in enumerate(POOL_WINDOWS):
        lanes = slice(g * gd, (g + 1) * gd)
        win = (lva_ref if (g + 1) % 2 else lvb_ref)[POOL_HALO:end, lanes]
        cnt = jnp.minimum(t + 1, w).astype(F32)
        pooled = win / cnt - u[:, lanes]
        mixed = jnp.dot(pooled.astype(BF16), pw_ref[g], preferred_element_type=F32)
        pool_ref[0, :, lanes] = (mixed * ps_ref[:, lanes]).astype(BF16)
    ubuf_ref[0:POOL_HALO, :] = ubuf_ref[tm:end, :]

    v = jnp.dot(hb, w_ref[:, v0:v0 + attn_width], preferred_element_type=F32)
    lane = lax.broadcasted_iota(jnp.int32, (1, LANES), 1)
    for pr in range(attn_width // LANES):
        pair = v[:, pr * LANES:(pr + 1) * LANES]
        for hh in range(2):
            keep = (lane < HEAD_DIM) if hh == 0 else (lane >= HEAD_DIM)
            ones = jnp.where(lane == _ones_lane(hh), 1.0, 0.0)
            h = 2 * pr + hh
            v_ref[0, :, h * LANES:(h + 1) * LANES] = jnp.where(keep, pair, ones).astype(BF16)


def _ones_lane(hh):
    return HEAD_DIM if hh == 0 else 0


def _in_proj(x, g, w_in, pool_w, pool_scale, qg, kg, cos, sin, *, tm):
    B, S, D = x.shape
    n_groups, gd, _ = pool_w.shape
    pool_width = n_groups * gd
    attn_width = (w_in.shape[1] - pool_width) // 3
    nblk = S // MOBA_BLOCK
    kern = functools.partial(_in_proj_kernel, tm=tm, pool_width=pool_width, attn_width=attn_width)
    const2 = lambda b, s: (0, 0)
    tile3 = lambda b, s: (b, s, 0)
    return pl.pallas_call(
        kern,
        grid=(B, S // tm),
        in_specs=[
            pl.BlockSpec((1, tm, D), tile3),
            pl.BlockSpec((1, D), const2),
            pl.BlockSpec(w_in.shape, const2),
            pl.BlockSpec(pool_w.shape, lambda b, s: (0, 0, 0)),
            pl.BlockSpec((1, pool_width), const2),
            pl.BlockSpec((1, LANES), const2),
            pl.BlockSpec((1, LANES), const2),
            pl.BlockSpec((tm, LANES), lambda b, s: (s, 0)),
            pl.BlockSpec((tm, LANES), lambda b, s: (s, 0)),
        ],
        out_specs=[
            pl.BlockSpec((1, tm, pool_width), tile3),
            pl.BlockSpec((1, tm, attn_width), tile3),
            pl.BlockSpec((1, tm, attn_width), tile3),
            pl.BlockSpec((1, tm, 2 * attn_width), tile3),
            pl.BlockSpec((1, tm // MOBA_BLOCK, 1, attn_width), lambda b, s: (b, s, 0, 0)),
        ],
        out_shape=[
            jax.ShapeDtypeStruct((B, S, pool_width), BF16),
            jax.ShapeDtypeStruct((B, S, attn_width), BF16),
            jax.ShapeDtypeStruct((B, S, attn_width), BF16),
            jax.ShapeDtypeStruct((B, S, 2 * attn_width), BF16),
            jax.ShapeDtypeStruct((B, nblk, 1, attn_width), F32),
        ],
        scratch_shapes=[pltpu.VMEM((POOL_HALO + tm, pool_width), F32)] * 3 + [pltpu.VMEM((tm, 2 * attn_width), F32)],
        compiler_params=pltpu.CompilerParams(
            dimension_semantics=("arbitrary", "arbitrary"), vmem_limit_bytes=VMEM_LIMIT),
        name="in_proj",
    )(x, g, w_in, pool_w, pool_scale, qg, kg, cos, sin)


def _moba_kernel(q_ref, k_ref, v_ref, kmean_ref, o_ref, qh_ref, sel_ref, s_ref, m_ref, acc_ref, *, nblk, npairs):
    mstep = pl.program_id(2)
    blk = MOBA_BLOCK
    nheads = 2 * npairs
    both = (0, 1)
    first = _first_head_lanes()
    bidx = lax.broadcasted_iota(jnp.int32, (nblk, 1), 0)

    for qb in both:
        eligible = bidx < 2 * mstep + qb
        for pr in range(npairs):
            lanes = slice(pr * LANES, (pr + 1) * LANES)
            qs = q_ref[0, qb * blk:(qb + 1) * blk, lanes]
            zero = jnp.zeros_like(qs)
            km = kmean_ref[0, :, lanes]
            km_hi = km.astype(BF16)
            km_split = jnp.concatenate([km_hi, (km - km_hi.astype(F32)).astype(BF16)], axis=0)
            for hh in range(2):
                st = qb * nheads + 2 * pr + hh
                qm = jnp.where(first, qs, zero) if hh == 0 else jnp.where(first, zero, qs)
                qh_ref[st] = qm.T
                split = jnp.dot(km_split, qh_ref[st], preferred_element_type=F32)
                gate = jnp.where(eligible, split[:nblk] + split[nblk:], -jnp.inf)
                picked = jnp.zeros((nblk, blk), jnp.bool_)
                for _ in range(MOBA_TOPK):
                    best = jnp.max(gate, axis=0, keepdims=True)
                    where_best = jnp.min(jnp.where(gate == best, bidx, nblk), axis=0, keepdims=True)
                    hit = bidx == where_best
                    picked = picked | hit
                    gate = jnp.where(hit, -jnp.inf, gate)
                sel = jnp.where(picked & eligible, 1.0, 0.0)
                for j in range(nblk):
                    sel_ref[st, j] = sel[j:j + 1, :]

    m_ref[...] = jnp.full(m_ref.shape, NEG, F32)
    acc_ref[...] = jnp.zeros_like(acc_ref)

    def scores(slot, kblock, qbs):
        off = pl.multiple_of(kblock * blk, blk)
        for pr in range(npairs):
            kb = k_ref[0, pl.ds(off, blk), pr * LANES:(pr + 1) * LANES]
            for qb in qbs:
                for hh in range(2):
                    st = qb * nheads + 2 * pr + hh
                    s_ref[slot, st] = jnp.dot(kb, qh_ref[st], preferred_element_type=F32)

    def accumulate(slot, kblock, past, qbs):
        off = pl.multiple_of(kblock * blk, blk)
        for qb in qbs:
            for pr in range(npairs):
                for hh in range(2):
                    h = 2 * pr + hh
                    st = qb * nheads + h
                    vb = v_ref[0, pl.ds(off, blk), h * LANES:(h + 1) * LANES]
                    sc = s_ref[slot, st]
                    m = m_ref[st]
                    if not past:
                        causal = (lax.broadcasted_iota(jnp.int32, (blk, blk), 0)
                                  <= lax.broadcasted_iota(jnp.int32, (blk, blk), 1))
                        sc = jnp.where(causal, sc, NEG)
                        m_new = jnp.maximum(m, jnp.max(sc, axis=0, keepdims=True))
                        p = jnp.exp2(sc - m_new)
                    else:
                        picked = sel_ref[st, kblock] > 0.0
                        m_new = jnp.maximum(m, jnp.where(picked, jnp.max(sc, axis=0, keepdims=True), NEG))
                        p = jnp.exp2(sc - jnp.where(picked, m_new, -NEG))
                    alpha = jnp.exp2(m - m_new)
                    m_ref[st] = m_new
                    pv = lax.dot_general(vb, p.astype(BF16), TN_DIMS, preferred_element_type=F32)
                    ones_row = _ones_lane(hh)
                    acc_ref[st, 0:HEAD_DIM] = alpha * acc_ref[st, 0:HEAD_DIM] + pv[hh * HEAD_DIM:(hh + 1) * HEAD_DIM]
                    acc_ref[st, HEAD_DIM:] = alpha * acc_ref[st, HEAD_DIM:] + pv[ones_row:ones_row + SUBLANES]

    scores(0, 0, both)

    def pair(t, _):
        a = 2 * t
        scores(1, a + 1, both)
        accumulate(0, a, True, both)
        scores(0, a + 2, both)
        accumulate(1, a + 1, True, both)
        return 0

    lax.fori_loop(0, mstep, pair, 0)
    scores(1, 2 * mstep + 1, (1,))
    accumulate(0, 2 * mstep, False, (0,))
    accumulate(0, 2 * mstep, True, (1,))
    accumulate(1, 2 * mstep + 1, False, (1,))

    for qb in both:
        for pr in range(npairs):
            sts = (qb * nheads + 2 * pr, qb * nheads + 2 * pr + 1)
            halves = [acc_ref[st, 0:HEAD_DIM] / acc_ref[st, HEAD_DIM:HEAD_DIM + 1] for st in sts]
            o_ref[0, qb * blk:(qb + 1) * blk, pr * LANES:(pr + 1) * LANES] = (
                jnp.concatenate(halves, axis=0).T.astype(BF16))


def _moba(q, k, v, kmean, *, npairs):
    B, S, W = q.shape
    nblk = S // MOBA_BLOCK
    wstep = npairs * LANES
    nstreams = 2 * 2 * npairs
    kern = functools.partial(_moba_kernel, nblk=nblk, npairs=npairs)
    return pl.pallas_call(
        kern,
        grid=(B, W // wstep, nblk // 2),
        in_specs=[
            pl.BlockSpec((1, 2 * MOBA_BLOCK, wstep), lambda b, g, i: (b, i, g)),
            pl.BlockSpec((1, S, wstep), lambda b, g, i: (b, 0, g)),
            pl.BlockSpec((1, S, 2 * wstep), lambda b, g, i: (b, 0, g)),
            pl.BlockSpec((1, nblk, wstep), lambda b, g, i: (b, 0, g)),
        ],
        out_specs=pl.BlockSpec((1, 2 * MOBA_BLOCK, wstep), lambda b, g, i: (b, i, g)),
        out_shape=jax.ShapeDtypeStruct((B, S, W), BF16),
        scratch_shapes=[pltpu.VMEM((nstreams, LANES, MOBA_BLOCK), BF16),
                        pltpu.VMEM((nstreams, nblk, 1, MOBA_BLOCK), F32),
                        pltpu.VMEM((2, nstreams, MOBA_BLOCK, MOBA_BLOCK), F32),
                        pltpu.VMEM((nstreams, 1, MOBA_BLOCK), F32),
                        pltpu.VMEM((nstreams, HEAD_DIM + SUBLANES, MOBA_BLOCK), F32)],
        compiler_params=pltpu.CompilerParams(
            dimension_semantics=("arbitrary", "arbitrary", "arbitrary"), vmem_limit_bytes=VMEM_LIMIT),
        name="moba",
    )(q, k, v, kmean)


def _mem_kv_kernel(mem_ref, g_ref, w_ref, kg_ref, k_ref, v_ref, *, d_model):
    mb = _rms(mem_ref[0], g_ref[...]).astype(BF16)
    hd = d_model // XATTN_HEADS
    for h in range(XATTN_HEADS):
        lanes = slice(h * hd, (h + 1) * hd)
        kh = jnp.dot(mb, w_ref[:, lanes], preferred_element_type=F32)
        k_ref[0, :, lanes] = _rms(kh, kg_ref[...]).astype(BF16)
    v_ref[0] = jnp.dot(mb, w_ref[:, d_model:2 * d_model], preferred_element_type=F32).astype(BF16)


def _mem_kv(mem, g, w_xkv, kg):
    B, M, D = mem.shape
    const2 = lambda b: (0, 0)
    blk3 = lambda b: (b, 0, 0)
    return pl.pallas_call(
        functools.partial(_mem_kv_kernel, d_model=D),
        grid=(B,),
        in_specs=[pl.BlockSpec((1, M, D), blk3), pl.BlockSpec((1, D), const2),
                  pl.BlockSpec(w_xkv.shape, const2), pl.BlockSpec((1, D // XATTN_HEADS), const2)],
        out_specs=[pl.BlockSpec((1, M, D), blk3), pl.BlockSpec((1, M, D), blk3)],
        out_shape=[jax.ShapeDtypeStruct((B, M, D), BF16)] * 2,
        compiler_params=pltpu.CompilerParams(
            dimension_semantics=("arbitrary",), vmem_limit_bytes=VMEM_LIMIT),
        name="mem_kv",
    )(mem, g, w_xkv, kg)


def _mix_xattn_kernel(x_ref, pool_ref, attn_ref, wo_ref, g_ref, wq_ref, qg_ref, k_ref, v_ref, wxo_ref,
                      o_ref, q_ref, qn_ref, s_ref, p_ref, oh_ref, *, pool_width, d_model):
    x1 = (x_ref[0]
          + jnp.dot(pool_ref[0], wo_ref[0:pool_width, :], preferred_element_type=F32)
          + jnp.dot(attn_ref[0], wo_ref[pool_width:, :], preferred_element_type=F32))
    hb = _rms(x1, g_ref[...]).astype(BF16)
    hd = d_model // XATTN_HEADS
    heads = [slice(h * hd, (h + 1) * hd) for h in range(XATTN_HEADS)]
    q_ref[...] = jnp.dot(hb, wq_ref[...], preferred_element_type=F32)
    for lanes in heads:
        qn_ref[:, lanes] = (_rms(q_ref[:, lanes], qg_ref[...]) * (hd ** -0.5)).astype(BF16)
    for h, lanes in enumerate(heads):
        s_ref[h] = lax.dot_general(qn_ref[:, lanes], k_ref[0, :, lanes], NT_DIMS,
                                   preferred_element_type=F32)
    for h in range(XATTN_HEADS):
        s = s_ref[h]
        p = jnp.exp(s - jnp.max(s, axis=-1, keepdims=True))
        p_ref[h] = (p / jnp.sum(p, axis=-1, keepdims=True)).astype(BF16)
    for h, lanes in enumerate(heads):
        oh_ref[:, lanes] = jnp.dot(p_ref[h], v_ref[0, :, lanes], preferred_element_type=F32).astype(BF16)
    o_ref[0] = x1 + jnp.dot(oh_ref[...], wxo_ref[...], preferred_element_type=F32)


def _mix_xattn(x, pool, attn, w_out, g, w_xq, qg, kx, vx, w_xo, *, tm):
    B, S, D = x.shape
    M = kx.shape[1]
    pool_width = pool.shape[2]
    const2 = lambda b, s: (0, 0)
    tile3 = lambda b, s: (b, s, 0)
    mem3 = lambda b, s: (b, 0, 0)
    return pl.pallas_call(
        functools.partial(_mix_xattn_kernel, pool_width=pool_width, d_model=D),
        grid=(B, S // tm),
        in_specs=[
            pl.BlockSpec((1, tm, D), tile3),
            pl.BlockSpec((1, tm, pool_width), tile3),
            pl.BlockSpec((1, tm, attn.shape[2]), tile3),
            pl.BlockSpec(w_out.shape, const2),
            pl.BlockSpec((1, D), const2),
            pl.BlockSpec(w_xq.shape, const2),
            pl.BlockSpec((1, D // XATTN_HEADS), const2),
            pl.BlockSpec((1, M, D), mem3),
            pl.BlockSpec((1, M, D), mem3),
            pl.BlockSpec(w_xo.shape, const2),
        ],
        out_specs=pl.BlockSpec((1, tm, D), tile3),
        out_shape=jax.ShapeDtypeStruct((B, S, D), F32),
        scratch_shapes=[pltpu.VMEM((tm, D), F32), pltpu.VMEM((tm, D), BF16),
                        pltpu.VMEM((XATTN_HEADS, tm, M), F32), pltpu.VMEM((XATTN_HEADS, tm, M), BF16),
                        pltpu.VMEM((tm, D), BF16)],
        compiler_params=pltpu.CompilerParams(
            dimension_semantics=("arbitrary", "arbitrary"), vmem_limit_bytes=VMEM_LIMIT),
        name="mix_xattn",
    )(x, pool, attn, w_out, g, w_xq, qg, kx, vx, w_xo)


def _conv_ffn_kernel(x_ref, g_ref, wu_ref, cw_ref, cb_ref, wd_ref, o_ref,
                     hb_ref, halo_ref, buf_ref, act_ref, *, tm, n_chunks):
    s = pl.program_id(1)

    @pl.when(s == 0)
    def _():
        halo_ref[...] = jnp.zeros_like(halo_ref)

    x = x_ref[0]
    hb_ref[...] = _rms(x, g_ref[...]).astype(BF16)

    def conv(c, slot):
        cols = slice(c * FF_CHUNK, (c + 1) * FF_CHUNK)
        up = jnp.dot(hb_ref[...], wu_ref[:, cols], preferred_element_type=F32)
        buf_ref[slot, 0:CONV_HALO, :] = halo_ref[c]
        buf_ref[slot, CONV_HALO:CONV_HALO + tm, :] = up
        halo_ref[c] = up[tm - CONV_HALO:, :]
        cw = cw_ref[:, cols]
        y = up * cw[CONV_WIDTH - 1:CONV_WIDTH, :] + cb_ref[:, cols]
        for d in range(1, CONV_WIDTH):
            y = y + buf_ref[slot, CONV_HALO - d:CONV_HALO - d + tm, :] * cw[CONV_WIDTH - 1 - d:CONV_WIDTH - d, :]
        return y

    for c in range(n_chunks):
        gate = conv(c, 2 * (c % 2))
        val = conv(c + n_chunks, 2 * (c % 2) + 1)
        act_ref[:, c * FF_CHUNK:(c + 1) * FF_CHUNK] = (gate * (1.0 / (1.0 + jnp.exp2(gate * -LOG2E))) * val).astype(BF16)
    o_ref[0] = x + jnp.dot(act_ref[...], wd_ref[...], preferred_element_type=F32)


def _conv_ffn(x, g, w_up, conv_w, conv_b, w_down, *, tm):
    B, S, D = x.shape
    d_ff = w_down.shape[0]
    n_chunks = d_ff // FF_CHUNK
    const2 = lambda b, s: (0, 0)
    tile3 = lambda b, s: (b, s, 0)
    return pl.pallas_call(
        functools.partial(_conv_ffn_kernel, tm=tm, n_chunks=n_chunks),
        grid=(B, S // tm),
        in_specs=[
            pl.BlockSpec((1, tm, D), tile3),
            pl.BlockSpec((1, D), const2),
            pl.BlockSpec(w_up.shape, const2),
            pl.BlockSpec(conv_w.shape, const2),
            pl.BlockSpec(conv_b.shape, const2),
            pl.BlockSpec(w_down.shape, const2),
        ],
        out_specs=pl.BlockSpec((1, tm, D), tile3),
        out_shape=jax.ShapeDtypeStruct((B, S, D), F32),
        scratch_shapes=[
            pltpu.VMEM((tm, D), BF16),
            pltpu.VMEM((2 * n_chunks, CONV_HALO, FF_CHUNK), F32),
            pltpu.VMEM((4, CONV_HALO + tm, FF_CHUNK), F32),
            pltpu.VMEM((tm, d_ff), BF16),
        ],
        compiler_params=pltpu.CompilerParams(
            dimension_semantics=("arbitrary", "arbitrary"), vmem_limit_bytes=VMEM_LIMIT),
        name="conv_ffn",
    )(x, g, w_up, conv_w, conv_b, w_down)


def _rope_tables(S):
    half = HEAD_DIM // 2
    inv_freq = ROPE_THETA ** (-jnp.arange(half, dtype=F32) / half)
    ang = jnp.arange(S).astype(F32)[:, None] * inv_freq[None, :]
    cos, sin = jnp.cos(ang), jnp.sin(ang)
    return (jnp.concatenate([cos, cos, cos, cos], axis=-1),
            jnp.concatenate([-sin, -sin, sin, sin], axis=-1))


def _slab_gain(g):
    lo, hi = g[:HEAD_DIM // 2], g[HEAD_DIM // 2:]
    return jnp.concatenate([lo, lo, hi, hi]).reshape(1, LANES)


def _slab_columns(w):
    rows, width = w.shape
    w = w.reshape(rows, width // LANES, 2, 2, HEAD_DIM // 2)
    return w.transpose(0, 1, 3, 2, 4).reshape(rows, width)


def kernel(x, mem, norm_mix_g, w_in, pool_w, pool_scale, q_norm_g, k_norm_g, w_out, norm_xattn_g, norm_mem_g, w_xq, w_xkv, xq_norm_g, xk_norm_g, w_xo, norm_ffn_g, w_up, conv_w, conv_b, w_down):
    B, S, D = x.shape
    depth = w_in.shape[0]
    d_ff = w_down.shape[1]
    assert all(S % t == 0 for t in (IN_PROJ_TILE, MIX_TILE, FFN_TILE, MOBA_BLOCK)) and d_ff % FF_CHUNK == 0
    cos, sin = _rope_tables(S)
    row = lambda a: a.reshape(1, -1)
    pool_width = pool_w.shape[1] * pool_w.shape[2]
    attn_width = (w_in.shape[2] - pool_width) // 3
    q0, k0, v0 = pool_width, pool_width + attn_width, pool_width + 2 * attn_width
    for l in range(depth):
        w_in_l = jnp.concatenate([w_in[l][:, :q0], _slab_columns(w_in[l][:, q0:k0]),
                                  _slab_columns(w_in[l][:, k0:v0]), w_in[l][:, v0:]], axis=1).astype(BF16)
        pool, q, k, v, kmean = _in_proj(
            x, row(norm_mix_g[l]), w_in_l, pool_w[l].astype(BF16), row(pool_scale[l]),
            _slab_gain(q_norm_g[l]), _slab_gain(k_norm_g[l]), cos, sin, tm=IN_PROJ_TILE)
        attn = _moba(q, k, v, kmean.reshape(B, S // MOBA_BLOCK, -1), npairs=attn_width // LANES)
        kx, vx = _mem_kv(mem, row(norm_mem_g[l]), w_xkv[l].astype(BF16), row(xk_norm_g[l]))
        x = _mix_xattn(x, pool, attn, w_out[l].astype(BF16), row(norm_xattn_g[l]), w_xq[l].astype(BF16),
                       row(xq_norm_g[l]), kx, vx, w_xo[l].astype(BF16), tm=MIX_TILE)
        x = _conv_ffn(x, row(norm_ffn_g[l]), w_up[l].astype(BF16), conv_w[l], row(conv_b[l]), w_down[l].astype(BF16),
                      tm=FFN_TILE)
    return x
```

```python
import functools

import jax
import jax.numpy as jnp
from jax import lax
from jax.experimental import pallas as pl
from jax.experimental.pallas import tpu as pltpu

F32 = jnp.float32
BF16 = jnp.bfloat16

EPS = 1e-6
LANES = 128
SUBLANES = 8
HEAD_DIM = 64
POOL_WINDOWS = (2, 4, 8, 16)
POOL_HALO = SUBLANES * len(POOL_WINDOWS)
MOBA_BLOCK = 256
MOBA_TOPK = 3
MOBA_QBLOCKS = 4
MOBA_PAIRS = 2
ROPE_THETA = 10000.0
XATTN_HEADS = 4
CONV_WIDTH = 3
CONV_HALO = 8
FF_CHUNK = 256
NEG = -1e30
LOG2E = 1.4426950408889634
VMEM_LIMIT = 56 * 1024 * 1024
IN_PROJ_TILE = 512
MIX_TILE = 1024
FFN_TILE = 512

NT_DIMS = (((1,), (1,)), ((), ()))
TN_DIMS = (((0,), (0,)), ((), ()))


def _rms(x, g):
    return x * lax.rsqrt(jnp.mean(x * x, axis=-1, keepdims=True) + EPS) * g


def _first_head_lanes():
    lane = lax.broadcasted_iota(jnp.int32, (1, LANES), 1)
    return (lane & (HEAD_DIM // 2)) == 0


def _head_norm_rope(y, g, cos, sin_signed):
    first = _first_head_lanes()
    sq = y * y
    s0 = jnp.sum(jnp.where(first, sq, 0.0), axis=-1, keepdims=True)
    s1 = jnp.sum(jnp.where(first, 0.0, sq), axis=-1, keepdims=True)
    r = lax.rsqrt(jnp.where(first, s0, s1) * (1.0 / HEAD_DIM) + EPS)
    yn = y * r * g
    return yn * cos + pltpu.roll(yn, LANES // 2, axis=1) * sin_signed


def _in_proj_kernel(x_ref, g_ref, w_ref, pw_ref, ps_ref, qg_ref, kg_ref, cos_ref, sin_ref,
                    pool_ref, q_ref, k_ref, v_ref, kmean_ref, ubuf_ref, lva_ref, lvb_ref, y_ref,
                    *, tm, pool_width, attn_width):
    s = pl.program_id(1)

    @pl.when(s == 0)
    def _():
        ubuf_ref[0:POOL_HALO, :] = jnp.zeros((POOL_HALO, pool_width), F32)

    hb = _rms(x_ref[0], g_ref[...]).astype(BF16)
    q0, v0 = pool_width, pool_width + 2 * attn_width

    y_ref[...] = jnp.dot(hb, w_ref[:, q0:v0], preferred_element_type=F32)
    cos = cos_ref[...]
    sin = sin_ref[...]
    qg = qg_ref[...] * (HEAD_DIM ** -0.5 * LOG2E)
    for c in range(attn_width // LANES):
        lanes = slice(c * LANES, (c + 1) * LANES)
        q_ref[0, :, lanes] = _head_norm_rope(y_ref[:, lanes], qg, cos, sin).astype(BF16)
        kr = _head_norm_rope(y_ref[:, attn_width + c * LANES:attn_width + (c + 1) * LANES], kg_ref[...], cos, sin)
        k_ref[0, :, lanes] = kr.astype(BF16)
        for r in range(tm // MOBA_BLOCK):
            km = jnp.sum(kr[r * MOBA_BLOCK:(r + 1) * MOBA_BLOCK], axis=0, keepdims=True) * (1.0 / MOBA_BLOCK)
            kmean_ref[0, r, :, lanes] = km

    u = jnp.dot(hb, w_ref[:, 0:pool_width], preferred_element_type=F32)
    end = POOL_HALO + tm
    ubuf_ref[POOL_HALO:end, :] = u
    gd = pool_width // len(POOL_WINDOWS)
    src = ubuf_ref
    for k in range(1, len(POOL_WINDOWS) + 1):
        dst = lva_ref if k % 2 else lvb_ref
        assert POOL_WINDOWS[k - 1] == 2 ** k
        shift, r0, l0 = 2 ** (k - 1), SUBLANES * k, (k - 1) * gd
        dst[r0:end, l0:] = src[r0:end, l0:] + src[r0 - shift:end - shift, l0:]
        src = dst
    t = s * tm + lax.broadcasted_iota(jnp.int32, (tm, 1), 0)
    for g, w in enumerate(POOL_WINDOWS):
        lanes = slice(g * gd, (g + 1) * gd)
        win = (lva_ref if (g + 1) % 2 else lvb_ref)[POOL_HALO:end, lanes]
        cnt = jnp.minimum(t + 1, w).astype(F32)
        pooled = win / cnt - u[:, lanes]
        mixed = jnp.dot(pooled.astype(BF16), pw_ref[g], preferred_element_type=F32)
        pool_ref[0, :, lanes] = (mixed * ps_ref[:, lanes]).astype(BF16)
    ubuf_ref[0:POOL_HALO, :] = ubuf_ref[tm:end, :]

    v = jnp.dot(hb, w_ref[:, v0:v0 + attn_width], preferred_element_type=F32)
    lane = lax.broadcasted_iota(jnp.int32, (1, LANES), 1)
    for pr in range(attn_width // LANES):
        pair = v[:, pr * LANES:(pr + 1) * LANES]
        for hh in range(2):
            keep = (lane < HEAD_DIM) if hh == 0 else (lane >= HEAD_DIM)
            ones = jnp.where(lane == _ones_lane(hh), 1.0, 0.0)
            h = 2 * pr + hh
            v_ref[0, :, h * LANES:(h + 1) * LANES] = jnp.where(keep, pair, ones).astype(BF16)


def _ones_lane(hh):
    return HEAD_DIM if hh == 0 else 0


def _in_proj(x, g, w_in, pool_w, pool_scale, qg, kg, cos, sin, *, tm):
    B, S, D = x.shape
    n_groups, gd, _ = pool_w.shape
    pool_width = n_groups * gd
    attn_width = (w_in.shape[1] - pool_width) // 3
    nblk = S // MOBA_BLOCK
    kern = functools.partial(_in_proj_kernel, tm=tm, pool_width=pool_width, attn_width=attn_width)
    const2 = lambda b, s: (0, 0)
    tile3 = lambda b, s: (b, s, 0)
    return pl.pallas_call(
        kern,
        grid=(B, S // tm),
        in_specs=[
            pl.BlockSpec((1, tm, D), tile3),
            pl.BlockSpec((1, D), const2),
            pl.BlockSpec(w_in.shape, const2),
            pl.BlockSpec(pool_w.shape, lambda b, s: (0, 0, 0)),
            pl.BlockSpec((1, pool_width), const2),
            pl.BlockSpec((1, LANES), const2),
            pl.BlockSpec((1, LANES), const2),
            pl.BlockSpec((tm, LANES), lambda b, s: (s, 0)),
            pl.BlockSpec((tm, LANES), lambda b, s: (s, 0)),
        ],
        out_specs=[
            pl.BlockSpec((1, tm, pool_width), tile3),
            pl.BlockSpec((1, tm, attn_width), tile3),
            pl.BlockSpec((1, tm, attn_width), tile3),
            pl.BlockSpec((1, tm, 2 * attn_width), tile3),
            pl.BlockSpec((1, tm // MOBA_BLOCK, 1, attn_width), lambda b, s: (b, s, 0, 0)),
        ],
        out_shape=[
            jax.ShapeDtypeStruct((B, S, pool_width), BF16),
            jax.ShapeDtypeStruct((B, S, attn_width), BF16),
            jax.ShapeDtypeStruct((B, S, attn_width), BF16),
            jax.ShapeDtypeStruct((B, S, 2 * attn_width), BF16),
            jax.ShapeDtypeStruct((B, nblk, 1, attn_width), F32),
        ],
        scratch_shapes=[pltpu.VMEM((POOL_HALO + tm, pool_width), F32)] * 3 + [pltpu.VMEM((tm, 2 * attn_width), F32)],
        compiler_params=pltpu.CompilerParams(
            dimension_semantics=("arbitrary", "arbitrary"), vmem_limit_bytes=VMEM_LIMIT),
        name="in_proj",
    )(x, g, w_in, pool_w, pool_scale, qg, kg, cos, sin)


def _moba_kernel(q_ref, k_ref, v_ref, kmean_ref, o_ref, qh_ref, sel_ref, s_ref, m_ref, acc_ref, *, nblk, npairs):
    mstep = pl.program_id(2)
    blk = MOBA_BLOCK
    nheads = 2 * npairs
    both = tuple(range(MOBA_QBLOCKS))
    first = _first_head_lanes()
    bidx = lax.broadcasted_iota(jnp.int32, (nblk, 1), 0)

    for qb in both:
        eligible = bidx < MOBA_QBLOCKS * mstep + qb
        for pr in range(npairs):
            lanes = slice(pr * LANES, (pr + 1) * LANES)
            qs = q_ref[0, qb * blk:(qb + 1) * blk, lanes]
            zero = jnp.zeros_like(qs)
            km = kmean_ref[0, :, lanes]
            km_hi = km.astype(BF16)
            km_split = jnp.concatenate([km_hi, (km - km_hi.astype(F32)).astype(BF16)], axis=0)
            for hh in range(2):
                st = qb * nheads + 2 * pr + hh
                qm = jnp.where(first, qs, zero) if hh == 0 else jnp.where(first, zero, qs)
                qh_ref[st] = qm.T
                split = jnp.dot(km_split, qh_ref[st], preferred_element_type=F32)
                gate = jnp.where(eligible, split[:nblk] + split[nblk:], -jnp.inf)
                picked = jnp.zeros((nblk, blk), jnp.bool_)
                for _ in range(MOBA_TOPK):
                    best = jnp.max(gate, axis=0, keepdims=True)
                    where_best = jnp.min(jnp.where(gate == best, bidx, nblk), axis=0, keepdims=True)
                    hit = bidx == where_best
                    picked = picked | hit
                    gate = jnp.where(hit, -jnp.inf, gate)
                sel = jnp.where(picked & eligible, 1.0, 0.0)
                for j in range(nblk):
                    sel_ref[st, j] = sel[j:j + 1, :]

    m_ref[...] = jnp.full(m_ref.shape, NEG, F32)
    acc_ref[...] = jnp.zeros_like(acc_ref)

    def scores(slot, kblock, qbs):
        off = pl.multiple_of(kblock * blk, blk)
        for pr in range(npairs):
            kb = k_ref[0, pl.ds(off, blk), pr * LANES:(pr + 1) * LANES]
            for qb in qbs:
                for hh in range(2):
                    st = qb * nheads + 2 * pr + hh
                    s_ref[slot, st] = jnp.dot(kb, qh_ref[st], preferred_element_type=F32)

    def accumulate(slot, kblock, past, qbs):
        off = pl.multiple_of(kblock * blk, blk)
        for qb in qbs:
            for pr in range(npairs):
                for hh in range(2):
                    h = 2 * pr + hh
                    st = qb * nheads + h
                    vb = v_ref[0, pl.ds(off, blk), h * LANES:(h + 1) * LANES]
                    sc = s_ref[slot, st]
                    m = m_ref[st]
                    if not past:
                        causal = (lax.broadcasted_iota(jnp.int32, (blk, blk), 0)
                                  <= lax.broadcasted_iota(jnp.int32, (blk, blk), 1))
                        sc = jnp.where(causal, sc, NEG)
                        m_new = jnp.maximum(m, jnp.max(sc, axis=0, keepdims=True))
                        p = jnp.exp2(sc - m_new)
                    else:
                        picked = sel_ref[st, kblock] > 0.0
                        m_new = jnp.maximum(m, jnp.where(picked, jnp.max(sc, axis=0, keepdims=True), NEG))
                        p = jnp.exp2(sc - jnp.where(picked, m_new, -NEG))
                    alpha = jnp.exp2(m - m_new)
                    m_ref[st] = m_new
                    pv = lax.dot_general(vb, p.astype(BF16), TN_DIMS, preferred_element_type=F32)
                    ones_row = _ones_lane(hh)
                    acc_ref[st, 0:HEAD_DIM] = alpha * acc_ref[st, 0:HEAD_DIM] + pv[hh * HEAD_DIM:(hh + 1) * HEAD_DIM]
                    acc_ref[st, HEAD_DIM:] = alpha * acc_ref[st, HEAD_DIM:] + pv[ones_row:ones_row + SUBLANES]

    scores(0, 0, both)

    def pair(t, _):
        a = 2 * t
        scores(1, a + 1, both)
        accumulate(0, a, True, both)
        scores(0, a + 2, both)
        accumulate(1, a + 1, True, both)
        return 0

    lax.fori_loop(0, (MOBA_QBLOCKS // 2) * mstep, pair, 0)
    base = MOBA_QBLOCKS * mstep
    for r in range(MOBA_QBLOCKS):
        later = tuple(range(r + 1, MOBA_QBLOCKS))
        if later:
            scores((r + 1) % 2, base + r + 1, later)
        accumulate(r % 2, base + r, False, (r,))
        if later:
            accumulate(r % 2, base + r, True, later)

    for qb in both:
        for pr in range(npairs):
            sts = (qb * nheads + 2 * pr, qb * nheads + 2 * pr + 1)
            halves = [acc_ref[st, 0:HEAD_DIM] / acc_ref[st, HEAD_DIM:HEAD_DIM + 1] for st in sts]
            o_ref[0, qb * blk:(qb + 1) * blk, pr * LANES:(pr + 1) * LANES] = (
                jnp.concatenate(halves, axis=0).T.astype(BF16))


def _moba(q, k, v, kmean, *, npairs):
    B, S, W = q.shape
    nblk = S // MOBA_BLOCK
    wstep = npairs * LANES
    nstreams = MOBA_QBLOCKS * 2 * npairs
    kern = functools.partial(_moba_kernel, nblk=nblk, npairs=npairs)
    return pl.pallas_call(
        kern,
        grid=(B, W // wstep, nblk // MOBA_QBLOCKS),
        in_specs=[
            pl.BlockSpec((1, MOBA_QBLOCKS * MOBA_BLOCK, wstep), lambda b, g, i: (b, i, g)),
            pl.BlockSpec((1, S, wstep), lambda b, g, i: (b, 0, g)),
            pl.BlockSpec((1, S, 2 * wstep), lambda b, g, i: (b, 0, g)),
            pl.BlockSpec((1, nblk, wstep), lambda b, g, i: (b, 0, g)),
        ],
        out_specs=pl.BlockSpec((1, MOBA_QBLOCKS * MOBA_BLOCK, wstep), lambda b, g, i: (b, i, g)),
        out_shape=jax.ShapeDtypeStruct((B, S, W), BF16),
        scratch_shapes=[pltpu.VMEM((nstreams, LANES, MOBA_BLOCK), BF16),
                        pltpu.VMEM((nstreams, nblk, 1, MOBA_BLOCK), F32),
                        pltpu.VMEM((2, nstreams, MOBA_BLOCK, MOBA_BLOCK), F32),
                        pltpu.VMEM((nstreams, 1, MOBA_BLOCK), F32),
                        pltpu.VMEM((nstreams, HEAD_DIM + SUBLANES, MOBA_BLOCK), F32)],
        compiler_params=pltpu.CompilerParams(
            dimension_semantics=("arbitrary", "arbitrary", "arbitrary"), vmem_limit_bytes=VMEM_LIMIT),
        name="moba",
    )(q, k, v, kmean)


def _mem_kv_kernel(mem_ref, g_ref, w_ref, kg_ref, k_ref, v_ref, *, d_model):
    mb = _rms(mem_ref[0], g_ref[...]).astype(BF16)
    hd = d_model // XATTN_HEADS
    for h in range(XATTN_HEADS):
        lanes = slice(h * hd, (h + 1) * hd)
        kh = jnp.dot(mb, w_ref[:, lanes], preferred_element_type=F32)
        k_ref[0, :, lanes] = _rms(kh, kg_ref[...]).astype(BF16)
    v_ref[0] = jnp.dot(mb, w_ref[:, d_model:2 * d_model], preferred_element_type=F32).astype(BF16)


def _mem_kv(mem, g, w_xkv, kg):
    B, M, D = mem.shape
    const2 = lambda b: (0, 0)
    blk3 = lambda b: (b, 0, 0)
    return pl.pallas_call(
        functools.partial(_mem_kv_kernel, d_model=D),
        grid=(B,),
        in_specs=[pl.BlockSpec((1, M, D), blk3), pl.BlockSpec((1, D), const2),
                  pl.BlockSpec(w_xkv.shape, const2), pl.BlockSpec((1, D // XATTN_HEADS), const2)],
        out_specs=[pl.BlockSpec((1, M, D), blk3), pl.BlockSpec((1, M, D), blk3)],
        out_shape=[jax.ShapeDtypeStruct((B, M, D), BF16)] * 2,
        compiler_params=pltpu.CompilerParams(
            dimension_semantics=("arbitrary",), vmem_limit_bytes=VMEM_LIMIT),
        name="mem_kv",
    )(mem, g, w_xkv, kg)


def _mix_xattn_kernel(x_ref, pool_ref, attn_ref, wo_ref, g_ref, wq_ref, qg_ref, k_ref, v_ref, wxo_ref,
                      o_ref, q_ref, qn_ref, s_ref, p_ref, oh_ref, *, pool_width, d_model):
    x1 = (x_ref[0]
          + jnp.dot(pool_ref[0], wo_ref[0:pool_width, :], preferred_element_type=F32)
          + jnp.dot(attn_ref[0], wo_ref[pool_width:, :], preferred_element_type=F32))
    hb = _rms(x1, g_ref[...]).astype(BF16)
    hd = d_model // XATTN_HEADS
    heads = [slice(h * hd, (h + 1) * hd) for h in range(XATTN_HEADS)]
    q_ref[...] = jnp.dot(hb, wq_ref[...], preferred_element_type=F32)
    for lanes in heads:
        qn_ref[:, lanes] = (_rms(q_ref[:, lanes], qg_ref[...]) * (hd ** -0.5)).astype(BF16)
    for h, lanes in enumerate(heads):
        s_ref[h] = lax.dot_general(qn_ref[:, lanes], k_ref[0, :, lanes], NT_DIMS,
                                   preferred_element_type=F32)
    for h in range(XATTN_HEADS):
        s = s_ref[h]
        p = jnp.exp(s - jnp.max(s, axis=-1, keepdims=True))
        p_ref[h] = (p / jnp.sum(p, axis=-1, keepdims=True)).astype(BF16)
    for h, lanes in enumerate(heads):
        oh_ref[:, lanes] = jnp.dot(p_ref[h], v_ref[0, :, lanes], preferred_element_type=F32).astype(BF16)
    o_ref[0] = x1 + jnp.dot(oh_ref[...], wxo_ref[...], preferred_element_type=F32)


def _mix_xattn(x, pool, attn, w_out, g, w_xq, qg, kx, vx, w_xo, *, tm):
    B, S, D = x.shape
    M = kx.shape[1]
    pool_width = pool.shape[2]
    const2 = lambda b, s: (0, 0)
    tile3 = lambda b, s: (b, s, 0)
    mem3 = lambda b, s: (b, 0, 0)
    return pl.pallas_call(
        functools.partial(_mix_xattn_kernel, pool_width=pool_width, d_model=D),
        grid=(B, S // tm),
        in_specs=[
            pl.BlockSpec((1, tm, D), tile3),
            pl.BlockSpec((1, tm, pool_width), tile3),
            pl.BlockSpec((1, tm, attn.shape[2]), tile3),
            pl.BlockSpec(w_out.shape, const2),
            pl.BlockSpec((1, D), const2),
            pl.BlockSpec(w_xq.shape, const2),
            pl.BlockSpec((1, D // XATTN_HEADS), const2),
            pl.BlockSpec((1, M, D), mem3),
            pl.BlockSpec((1, M, D), mem3),
            pl.BlockSpec(w_xo.shape, const2),
        ],
        out_specs=pl.BlockSpec((1, tm, D), tile3),
        out_shape=jax.ShapeDtypeStruct((B, S, D), F32),
        scratch_shapes=[pltpu.VMEM((tm, D), F32), pltpu.VMEM((tm, D), BF16),
                        pltpu.VMEM((XATTN_HEADS, tm, M), F32), pltpu.VMEM((XATTN_HEADS, tm, M), BF16),
                        pltpu.VMEM((tm, D), BF16)],
        compiler_params=pltpu.CompilerParams(
            dimension_semantics=("arbitrary", "arbitrary"), vmem_limit_bytes=VMEM_LIMIT),
        name="mix_xattn",
    )(x, pool, attn, w_out, g, w_xq, qg, kx, vx, w_xo)


def _conv_ffn_kernel(x_ref, g_ref, wu_ref, cw_ref, cb_ref, wd_ref, o_ref,
                     hb_ref, halo_ref, buf_ref, act_ref, *, tm, n_chunks):
    s = pl.program_id(1)

    @pl.when(s == 0)
    def _():
        halo_ref[...] = jnp.zeros_like(halo_ref)

    x = x_ref[0]
    hb_ref[...] = _rms(x, g_ref[...]).astype(BF16)

    def conv(c, slot):
        cols = slice(c * FF_CHUNK, (c + 1) * FF_CHUNK)
        up = jnp.dot(hb_ref[...], wu_ref[:, cols], preferred_element_type=F32)
        buf_ref[slot, 0:CONV_HALO, :] = halo_ref[c]
        buf_ref[slot, CONV_HALO:CONV_HALO + tm, :] = up
        halo_ref[c] = up[tm - CONV_HALO:, :]
        cw = cw_ref[:, cols]
        y = up * cw[CONV_WIDTH - 1:CONV_WIDTH, :] + cb_ref[:, cols]
        for d in range(1, CONV_WIDTH):
            y = y + buf_ref[slot, CONV_HALO - d:CONV_HALO - d + tm, :] * cw[CONV_WIDTH - 1 - d:CONV_WIDTH - d, :]
        return y

    for c in range(n_chunks):
        gate = conv(c, 2 * (c % 2))
        val = conv(c + n_chunks, 2 * (c % 2) + 1)
        act_ref[:, c * FF_CHUNK:(c + 1) * FF_CHUNK] = (gate * (1.0 / (1.0 + jnp.exp2(gate * -LOG2E))) * val).astype(BF16)
    o_ref[0] = x + jnp.dot(act_ref[...], wd_ref[...], preferred_element_type=F32)


def _conv_ffn(x, g, w_up, conv_w, conv_b, w_down, *, tm):
    B, S, D = x.shape
    d_ff = w_down.shape[0]
    n_chunks = d_ff // FF_CHUNK
    const2 = lambda b, s: (0, 0)
    tile3 = lambda b, s: (b, s, 0)
    return pl.pallas_call(
        functools.partial(_conv_ffn_kernel, tm=tm, n_chunks=n_chunks),
        grid=(B, S // tm),
        in_specs=[
            pl.BlockSpec((1, tm, D), tile3),
            pl.BlockSpec((1, D), const2),
            pl.BlockSpec(w_up.shape, const2),
            pl.BlockSpec(conv_w.shape, const2),
            pl.BlockSpec(conv_b.shape, const2),
            pl.BlockSpec(w_down.shape, const2),
        ],
        out_specs=pl.BlockSpec((1, tm, D), tile3),
        out_shape=jax.ShapeDtypeStruct((B, S, D), F32),
        scratch_shapes=[
            pltpu.VMEM((tm, D), BF16),
            pltpu.VMEM((2 * n_chunks, CONV_HALO, FF_CHUNK), F32),
            pltpu.VMEM((4, CONV_HALO + tm, FF_CHUNK), F32),
            pltpu.VMEM((tm, d_ff), BF16),
        ],
        compiler_params=pltpu.CompilerParams(
            dimension_semantics=("arbitrary", "arbitrary"), vmem_limit_bytes=VMEM_LIMIT),
        name="conv_ffn",
    )(x, g, w_up, conv_w, conv_b, w_down)


def _rope_tables(S):
    half = HEAD_DIM // 2
    inv_freq = ROPE_THETA ** (-jnp.arange(half, dtype=F32) / half)
    ang = jnp.arange(S).astype(F32)[:, None] * inv_freq[None, :]
    cos, sin = jnp.cos(ang), jnp.sin(ang)
    return (jnp.concatenate([cos, cos, cos, cos], axis=-1),
            jnp.concatenate([-sin, -sin, sin, sin], axis=-1))


def _slab_gain(g):
    lo, hi = g[:HEAD_DIM // 2], g[HEAD_DIM // 2:]
    return jnp.concatenate([lo, lo, hi, hi]).reshape(1, LANES)


def _slab_columns(w):
    rows, width = w.shape
    w = w.reshape(rows, width // LANES, 2, 2, HEAD_DIM // 2)
    return w.transpose(0, 1, 3, 2, 4).reshape(rows, width)


def kernel(x, mem, norm_mix_g, w_in, pool_w, pool_scale, q_norm_g, k_norm_g, w_out, norm_xattn_g, norm_mem_g, w_xq, w_xkv, xq_norm_g, xk_norm_g, w_xo, norm_ffn_g, w_up, conv_w, conv_b, w_down):
    B, S, D = x.shape
    depth = w_in.shape[0]
    d_ff = w_down.shape[1]
    assert all(S % t == 0 for t in (IN_PROJ_TILE, MIX_TILE, FFN_TILE, MOBA_QBLOCKS * MOBA_BLOCK)) and d_ff % FF_CHUNK == 0
    cos, sin = _rope_tables(S)
    row = lambda a: a.reshape(1, -1)
    pool_width = pool_w.shape[1] * pool_w.shape[2]
    attn_width = (w_in.shape[2] - pool_width) // 3
    q0, k0, v0 = pool_width, pool_width + attn_width, pool_width + 2 * attn_width
    for l in range(depth):
        w_in_l = jnp.concatenate([w_in[l][:, :q0], _slab_columns(w_in[l][:, q0:k0]),
                                  _slab_columns(w_in[l][:, k0:v0]), w_in[l][:, v0:]], axis=1).astype(BF16)
        pool, q, k, v, kmean = _in_proj(
            x, row(norm_mix_g[l]), w_in_l, pool_w[l].astype(BF16), row(pool_scale[l]),
            _slab_gain(q_norm_g[l]), _slab_gain(k_norm_g[l]), cos, sin, tm=IN_PROJ_TILE)
        attn = _moba(q, k, v, kmean.reshape(B, S // MOBA_BLOCK, -1), npairs=MOBA_PAIRS)
        kx, vx = _mem_kv(mem, row(norm_mem_g[l]), w_xkv[l].astype(BF16), row(xk_norm_g[l]))
        x = _mix_xattn(x, pool, attn, w_out[l].astype(BF16), row(norm_xattn_g[l]), w_xq[l].astype(BF16),
                       row(xq_norm_g[l]), kx, vx, w_xo[l].astype(BF16), tm=MIX_TILE)
        x = _conv_ffn(x, row(norm_ffn_g[l]), w_up[l].astype(BF16), conv_w[l], row(conv_b[l]), w_down[l].astype(BF16),
                      tm=FFN_TILE)
    return x
```

```python
import functools

import jax
import jax.numpy as jnp
from jax import lax
from jax.experimental import pallas as pl
from jax.experimental.pallas import tpu as pltpu

F32 = jnp.float32
BF16 = jnp.bfloat16

EPS = 1e-6
LANES = 128
SUBLANES = 8
HEAD_DIM = 64
POOL_WINDOWS = (2, 4, 8, 16)
POOL_HALO = SUBLANES * len(POOL_WINDOWS)
MOBA_BLOCK = 256
MOBA_TOPK = 3
MOBA_QBLOCKS = 4
MOBA_PAIRS = 2
ROPE_THETA = 10000.0
XATTN_HEADS = 4
CONV_WIDTH = 3
CONV_HALO = 8
FF_CHUNK = 256
NEG = -1e30
LOG2E = 1.4426950408889634
VMEM_LIMIT = 56 * 1024 * 1024
IN_PROJ_TILE = 1024
MIX_TILE = 1024
FFN_TILE = 512

NT_DIMS = (((1,), (1,)), ((), ()))
TN_DIMS = (((0,), (0,)), ((), ()))


def _rms(x, g):
    return x * lax.rsqrt(jnp.mean(x * x, axis=-1, keepdims=True) + EPS) * g


def _first_head_lanes():
    lane = lax.broadcasted_iota(jnp.int32, (1, LANES), 1)
    return (lane & (HEAD_DIM // 2)) == 0


def _head_norm_rope(y, g, cos, sin_signed):
    first = _first_head_lanes()
    sq = y * y
    s0 = jnp.sum(jnp.where(first, sq, 0.0), axis=-1, keepdims=True)
    s1 = jnp.sum(jnp.where(first, 0.0, sq), axis=-1, keepdims=True)
    r = lax.rsqrt(jnp.where(first, s0, s1) * (1.0 / HEAD_DIM) + EPS)
    yn = y * r * g
    return yn * cos + pltpu.roll(yn, LANES // 2, axis=1) * sin_signed


def _in_proj_kernel(x_ref, g_ref, w_ref, pw_ref, ps_ref, qg_ref, kg_ref, cos_ref, sin_ref,
                    pool_ref, q_ref, k_ref, v_ref, kmean_ref, ubuf_ref, lva_ref, lvb_ref, hb_ref, y_ref, vraw_ref,
                    *, tm, pool_width, attn_width):
    s = pl.program_id(1)
    half = tm // 2

    @pl.when(s == 0)
    def _():
        ubuf_ref[0:POOL_HALO, :] = jnp.zeros((POOL_HALO, pool_width), F32)

    q0, v0 = pool_width, pool_width + 2 * attn_width
    gd = pool_width // len(POOL_WINDOWS)
    cos_all, sin_all = cos_ref, sin_ref
    qg = qg_ref[...] * (HEAD_DIM ** -0.5 * LOG2E)

    def norm(h):
        rows = slice(h * half, (h + 1) * half)
        hb_ref[h] = _rms(x_ref[0, rows, :], g_ref[...]).astype(BF16)

    def project(h):
        hb = hb_ref[h]
        y_ref[h] = jnp.dot(hb, w_ref[:, q0:v0], preferred_element_type=F32)
        ubuf_ref[POOL_HALO + h * half:POOL_HALO + (h + 1) * half, :] = jnp.dot(
            hb, w_ref[:, 0:pool_width], preferred_element_type=F32)
        vraw_ref[h] = jnp.dot(hb, w_ref[:, v0:v0 + attn_width], preferred_element_type=F32)

    def finish(h):
        rows = slice(h * half, (h + 1) * half)
        cos = cos_all[rows, :]
        sin = sin_all[rows, :]
        for c in range(attn_width // LANES):
            lanes = slice(c * LANES, (c + 1) * LANES)
            q_ref[0, rows, lanes] = _head_norm_rope(y_ref[h, :, lanes], qg, cos, sin).astype(BF16)
            kr = _head_norm_rope(y_ref[h, :, attn_width + c * LANES:attn_width + (c + 1) * LANES], kg_ref[...], cos, sin)
            k_ref[0, rows, lanes] = kr.astype(BF16)
            for r in range(half // MOBA_BLOCK):
                km = jnp.sum(kr[r * MOBA_BLOCK:(r + 1) * MOBA_BLOCK], axis=0, keepdims=True) * (1.0 / MOBA_BLOCK)
                kmean_ref[0, h * (half // MOBA_BLOCK) + r, :, lanes] = km
        lo, end = h * half, POOL_HALO + (h + 1) * half
        src = ubuf_ref
        for k in range(1, len(POOL_WINDOWS) + 1):
            dst = lva_ref if k % 2 else lvb_ref
            assert POOL_WINDOWS[k - 1] == 2 ** k
            shift, r0, l0 = 2 ** (k - 1), lo + SUBLANES * k, (k - 1) * gd
            dst[r0:end, l0:] = src[r0:end, l0:] + src[r0 - shift:end - shift, l0:]
            src = dst
        t = s * tm + lo + lax.broadcasted_iota(jnp.int32, (half, 1), 0)
        for g, w in enumerate(POOL_WINDOWS):
            lanes = slice(g * gd, (g + 1) * gd)
            win = (lva_ref if (g + 1) % 2 else lvb_ref)[POOL_HALO + lo:end, lanes]
            cnt = jnp.minimum(t + 1, w).astype(F32)
            pooled = win / cnt - ubuf_ref[POOL_HALO + lo:end, lanes]
            mixed = jnp.dot(pooled.astype(BF16), pw_ref[g], preferred_element_type=F32)
            pool_ref[0, rows, lanes] = (mixed * ps_ref[:, lanes]).astype(BF16)
        lane = lax.broadcasted_iota(jnp.int32, (1, LANES), 1)
        for pr in range(attn_width // LANES):
            pair = vraw_ref[h, :, pr * LANES:(pr + 1) * LANES]
            for hh in range(2):
                keep = (lane < HEAD_DIM) if hh == 0 else (lane >= HEAD_DIM)
                ones = jnp.where(lane == _ones_lane(hh), 1.0, 0.0)
                hd = 2 * pr + hh
                v_ref[0, rows, hd * LANES:(hd + 1) * LANES] = jnp.where(keep, pair, ones).astype(BF16)

    norm(0)
    project(0)
    norm(1)
    finish(0)
    project(1)
    finish(1)
    ubuf_ref[0:POOL_HALO, :] = ubuf_ref[tm:tm + POOL_HALO, :]


def _ones_lane(hh):
    return HEAD_DIM if hh == 0 else 0


def _in_proj(x, g, w_in, pool_w, pool_scale, qg, kg, cos, sin, *, tm):
    B, S, D = x.shape
    n_groups, gd, _ = pool_w.shape
    pool_width = n_groups * gd
    attn_width = (w_in.shape[1] - pool_width) // 3
    nblk = S // MOBA_BLOCK
    kern = functools.partial(_in_proj_kernel, tm=tm, pool_width=pool_width, attn_width=attn_width)
    const2 = lambda b, s: (0, 0)
    tile3 = lambda b, s: (b, s, 0)
    return pl.pallas_call(
        kern,
        grid=(B, S // tm),
        in_specs=[
            pl.BlockSpec((1, tm, D), tile3),
            pl.BlockSpec((1, D), const2),
            pl.BlockSpec(w_in.shape, const2),
            pl.BlockSpec(pool_w.shape, lambda b, s: (0, 0, 0)),
            pl.BlockSpec((1, pool_width), const2),
            pl.BlockSpec((1, LANES), const2),
            pl.BlockSpec((1, LANES), const2),
            pl.BlockSpec((tm, LANES), lambda b, s: (s, 0)),
            pl.BlockSpec((tm, LANES), lambda b, s: (s, 0)),
        ],
        out_specs=[
            pl.BlockSpec((1, tm, pool_width), tile3),
            pl.BlockSpec((1, tm, attn_width), tile3),
            pl.BlockSpec((1, tm, attn_width), tile3),
            pl.BlockSpec((1, tm, 2 * attn_width), tile3),
            pl.BlockSpec((1, tm // MOBA_BLOCK, 1, attn_width), lambda b, s: (b, s, 0, 0)),
        ],
        out_shape=[
            jax.ShapeDtypeStruct((B, S, pool_width), BF16),
            jax.ShapeDtypeStruct((B, S, attn_width), BF16),
            jax.ShapeDtypeStruct((B, S, attn_width), BF16),
            jax.ShapeDtypeStruct((B, S, 2 * attn_width), BF16),
            jax.ShapeDtypeStruct((B, nblk, 1, attn_width), F32),
        ],
        scratch_shapes=[pltpu.VMEM((POOL_HALO + tm, pool_width), F32)] * 3 + [
            pltpu.VMEM((2, tm // 2, D), BF16), pltpu.VMEM((2, tm // 2, 2 * attn_width), F32),
            pltpu.VMEM((2, tm // 2, attn_width), F32)],
        compiler_params=pltpu.CompilerParams(
            dimension_semantics=("arbitrary", "arbitrary"), vmem_limit_bytes=VMEM_LIMIT),
        name="in_proj",
    )(x, g, w_in, pool_w, pool_scale, qg, kg, cos, sin)


def _moba_kernel(q_ref, k_ref, v_ref, kmean_ref, o_ref, qh_ref, sel_ref, s_ref, m_ref, acc_ref, *, nblk, npairs):
    mstep = pl.program_id(2)
    blk = MOBA_BLOCK
    nheads = 2 * npairs
    both = tuple(range(MOBA_QBLOCKS))
    first = _first_head_lanes()
    bidx = lax.broadcasted_iota(jnp.int32, (nblk, 1), 0)

    for qb in both:
        eligible = bidx < MOBA_QBLOCKS * mstep + qb
        for pr in range(npairs):
            lanes = slice(pr * LANES, (pr + 1) * LANES)
            qs = q_ref[0, qb * blk:(qb + 1) * blk, lanes]
            zero = jnp.zeros_like(qs)
            km = kmean_ref[0, :, lanes]
            km_hi = km.astype(BF16)
            km_split = jnp.concatenate([km_hi, (km - km_hi.astype(F32)).astype(BF16)], axis=0)
            for hh in range(2):
                st = qb * nheads + 2 * pr + hh
                qm = jnp.where(first, qs, zero) if hh == 0 else jnp.where(first, zero, qs)
                qh_ref[st] = qm.T
                split = jnp.dot(km_split, qh_ref[st], preferred_element_type=F32)
                gate = jnp.where(eligible, split[:nblk] + split[nblk:], -jnp.inf)
                picked = jnp.zeros((nblk, blk), jnp.bool_)
                for _ in range(MOBA_TOPK):
                    best = jnp.max(gate, axis=0, keepdims=True)
                    where_best = jnp.min(jnp.where(gate == best, bidx, nblk), axis=0, keepdims=True)
                    hit = bidx == where_best
                    picked = picked | hit
                    gate = jnp.where(hit, -jnp.inf, gate)
                sel = jnp.where(picked & eligible, 1.0, 0.0)
                for j in range(nblk):
                    sel_ref[st, j] = sel[j:j + 1, :]

    m_ref[...] = jnp.full(m_ref.shape, NEG, F32)
    acc_ref[...] = jnp.zeros_like(acc_ref)

    def scores(slot, kblock, qbs):
        off = pl.multiple_of(kblock * blk, blk)
        for pr in range(npairs):
            kb = k_ref[0, pl.ds(off, blk), pr * LANES:(pr + 1) * LANES]
            for qb in qbs:
                for hh in range(2):
                    st = qb * nheads + 2 * pr + hh
                    s_ref[slot, st] = jnp.dot(kb, qh_ref[st], preferred_element_type=F32)

    def accumulate(slot, kblock, past, qbs):
        off = pl.multiple_of(kblock * blk, blk)
        for qb in qbs:
            for pr in range(npairs):
                for hh in range(2):
                    h = 2 * pr + hh
                    st = qb * nheads + h
                    vb = v_ref[0, pl.ds(off, blk), h * LANES:(h + 1) * LANES]
                    sc = s_ref[slot, st]
                    m = m_ref[st]
                    if not past:
                        causal = (lax.broadcasted_iota(jnp.int32, (blk, blk), 0)
                                  <= lax.broadcasted_iota(jnp.int32, (blk, blk), 1))
                        sc = jnp.where(causal, sc, NEG)
                        m_new = jnp.maximum(m, jnp.max(sc, axis=0, keepdims=True))
                        p = jnp.exp2(sc - m_new)
                    else:
                        picked = sel_ref[st, kblock] > 0.0
                        m_new = jnp.maximum(m, jnp.where(picked, jnp.max(sc, axis=0, keepdims=True), NEG))
                        p = jnp.exp2(sc - jnp.where(picked, m_new, -NEG))
                    alpha = jnp.exp2(m - m_new)
                    m_ref[st] = m_new
                    pv = lax.dot_general(vb, p.astype(BF16), TN_DIMS, preferred_element_type=F32)
                    ones_row = _ones_lane(hh)
                    acc_ref[st, 0:HEAD_DIM] = alpha * acc_ref[st, 0:HEAD_DIM] + pv[hh * HEAD_DIM:(hh + 1) * HEAD_DIM]
                    acc_ref[st, HEAD_DIM:] = alpha * acc_ref[st, HEAD_DIM:] + pv[ones_row:ones_row + SUBLANES]

    scores(0, 0, both)

    def pair(t, _):
        a = 2 * t
        scores(1, a + 1, both)
        accumulate(0, a, True, both)
        scores(0, a + 2, both)
        accumulate(1, a + 1, True, both)
        return 0

    lax.fori_loop(0, (MOBA_QBLOCKS // 2) * mstep, pair, 0)
    base = MOBA_QBLOCKS * mstep
    for r in range(MOBA_QBLOCKS):
        later = tuple(range(r + 1, MOBA_QBLOCKS))
        if later:
            scores((r + 1) % 2, base + r + 1, later)
        accumulate(r % 2, base + r, False, (r,))
        if later:
            accumulate(r % 2, base + r, True, later)

    for qb in both:
        for pr in range(npairs):
            sts = (qb * nheads + 2 * pr, qb * nheads + 2 * pr + 1)
            halves = [acc_ref[st, 0:HEAD_DIM] / acc_ref[st, HEAD_DIM:HEAD_DIM + 1] for st in sts]
            o_ref[0, qb * blk:(qb + 1) * blk, pr * LANES:(pr + 1) * LANES] = (
                jnp.concatenate(halves, axis=0).T.astype(BF16))


def _moba(q, k, v, kmean, *, npairs):
    B, S, W = q.shape
    nblk = S // MOBA_BLOCK
    wstep = npairs * LANES
    nstreams = MOBA_QBLOCKS * 2 * npairs
    kern = functools.partial(_moba_kernel, nblk=nblk, npairs=npairs)
    return pl.pallas_call(
        kern,
        grid=(B, W // wstep, nblk // MOBA_QBLOCKS),
        in_specs=[
            pl.BlockSpec((1, MOBA_QBLOCKS * MOBA_BLOCK, wstep), lambda b, g, i: (b, i, g)),
            pl.BlockSpec((1, S, wstep), lambda b, g, i: (b, 0, g)),
            pl.BlockSpec((1, S, 2 * wstep), lambda b, g, i: (b, 0, g)),
            pl.BlockSpec((1, nblk, wstep), lambda b, g, i: (b, 0, g)),
        ],
        out_specs=pl.BlockSpec((1, MOBA_QBLOCKS * MOBA_BLOCK, wstep), lambda b, g, i: (b, i, g)),
        out_shape=jax.ShapeDtypeStruct((B, S, W), BF16),
        scratch_shapes=[pltpu.VMEM((nstreams, LANES, MOBA_BLOCK), BF16),
                        pltpu.VMEM((nstreams, nblk, 1, MOBA_BLOCK), F32),
                        pltpu.VMEM((2, nstreams, MOBA_BLOCK, MOBA_BLOCK), F32),
                        pltpu.VMEM((nstreams, 1, MOBA_BLOCK), F32),
                        pltpu.VMEM((nstreams, HEAD_DIM + SUBLANES, MOBA_BLOCK), F32)],
        compiler_params=pltpu.CompilerParams(
            dimension_semantics=("arbitrary", "arbitrary", "arbitrary"), vmem_limit_bytes=VMEM_LIMIT),
        name="moba",
    )(q, k, v, kmean)


def _mem_kv_kernel(mem_ref, g_ref, w_ref, kg_ref, k_ref, v_ref, *, d_model):
    mb = _rms(mem_ref[0], g_ref[...]).astype(BF16)
    hd = d_model // XATTN_HEADS
    for h in range(XATTN_HEADS):
        lanes = slice(h * hd, (h + 1) * hd)
        kh = jnp.dot(mb, w_ref[:, lanes], preferred_element_type=F32)
        k_ref[0, :, lanes] = _rms(kh, kg_ref[...]).astype(BF16)
    v_ref[0] = jnp.dot(mb, w_ref[:, d_model:2 * d_model], preferred_element_type=F32).astype(BF16)


def _mem_kv(mem, g, w_xkv, kg):
    B, M, D = mem.shape
    const2 = lambda b: (0, 0)
    blk3 = lambda b: (b, 0, 0)
    return pl.pallas_call(
        functools.partial(_mem_kv_kernel, d_model=D),
        grid=(B,),
        in_specs=[pl.BlockSpec((1, M, D), blk3), pl.BlockSpec((1, D), const2),
                  pl.BlockSpec(w_xkv.shape, const2), pl.BlockSpec((1, D // XATTN_HEADS), const2)],
        out_specs=[pl.BlockSpec((1, M, D), blk3), pl.BlockSpec((1, M, D), blk3)],
        out_shape=[jax.ShapeDtypeStruct((B, M, D), BF16)] * 2,
        compiler_params=pltpu.CompilerParams(
            dimension_semantics=("arbitrary",), vmem_limit_bytes=VMEM_LIMIT),
        name="mem_kv",
    )(mem, g, w_xkv, kg)


def _mix_xattn_kernel(x_ref, pool_ref, attn_ref, wo_ref, g_ref, wq_ref, qg_ref, k_ref, v_ref, wxo_ref,
                      o_ref, q_ref, qn_ref, s_ref, p_ref, oh_ref, *, pool_width, d_model):
    x1 = (x_ref[0]
          + jnp.dot(pool_ref[0], wo_ref[0:pool_width, :], preferred_element_type=F32)
          + jnp.dot(attn_ref[0], wo_ref[pool_width:, :], preferred_element_type=F32))
    hb = _rms(x1, g_ref[...]).astype(BF16)
    hd = d_model // XATTN_HEADS
    heads = [slice(h * hd, (h + 1) * hd) for h in range(XATTN_HEADS)]
    q_ref[...] = jnp.dot(hb, wq_ref[...], preferred_element_type=F32)
    for lanes in heads:
        qn_ref[:, lanes] = (_rms(q_ref[:, lanes], qg_ref[...]) * (hd ** -0.5)).astype(BF16)
    for h, lanes in enumerate(heads):
        s_ref[h] = lax.dot_general(qn_ref[:, lanes], k_ref[0, :, lanes], NT_DIMS,
                                   preferred_element_type=F32)
    for h in range(XATTN_HEADS):
        s = s_ref[h]
        p = jnp.exp(s - jnp.max(s, axis=-1, keepdims=True))
        p_ref[h] = (p / jnp.sum(p, axis=-1, keepdims=True)).astype(BF16)
    for h, lanes in enumerate(heads):
        oh_ref[:, lanes] = jnp.dot(p_ref[h], v_ref[0, :, lanes], preferred_element_type=F32).astype(BF16)
    o_ref[0] = x1 + jnp.dot(oh_ref[...], wxo_ref[...], preferred_element_type=F32)


def _mix_xattn(x, pool, attn, w_out, g, w_xq, qg, kx, vx, w_xo, *, tm):
    B, S, D = x.shape
    M = kx.shape[1]
    pool_width = pool.shape[2]
    const2 = lambda b, s: (0, 0)
    tile3 = lambda b, s: (b, s, 0)
    mem3 = lambda b, s: (b, 0, 0)
    return pl.pallas_call(
        functools.partial(_mix_xattn_kernel, pool_width=pool_width, d_model=D),
        grid=(B, S // tm),
        in_specs=[
            pl.BlockSpec((1, tm, D), tile3),
            pl.BlockSpec((1, tm, pool_width), tile3),
            pl.BlockSpec((1, tm, attn.shape[2]), tile3),
            pl.BlockSpec(w_out.shape, const2),
            pl.BlockSpec((1, D), const2),
            pl.BlockSpec(w_xq.shape, const2),
            pl.BlockSpec((1, D // XATTN_HEADS), const2),
            pl.BlockSpec((1, M, D), mem3),
            pl.BlockSpec((1, M, D), mem3),
            pl.BlockSpec(w_xo.shape, const2),
        ],
        out_specs=pl.BlockSpec((1, tm, D), tile3),
        out_shape=jax.ShapeDtypeStruct((B, S, D), F32),
        scratch_shapes=[pltpu.VMEM((tm, D), F32), pltpu.VMEM((tm, D), BF16),
                        pltpu.VMEM((XATTN_HEADS, tm, M), F32), pltpu.VMEM((XATTN_HEADS, tm, M), BF16),
                        pltpu.VMEM((tm, D), BF16)],
        compiler_params=pltpu.CompilerParams(
            dimension_semantics=("arbitrary", "arbitrary"), vmem_limit_bytes=VMEM_LIMIT),
        name="mix_xattn",
    )(x, pool, attn, w_out, g, w_xq, qg, kx, vx, w_xo)


def _conv_ffn_kernel(x_ref, g_ref, wu_ref, cw_ref, cb_ref, wd_ref, o_ref,
                     hb_ref, halo_ref, buf_ref, act_ref, *, tm, n_chunks):
    s = pl.program_id(1)

    @pl.when(s == 0)
    def _():
        halo_ref[...] = jnp.zeros_like(halo_ref)

    x = x_ref[0]
    hb_ref[...] = _rms(x, g_ref[...]).astype(BF16)

    def conv(c, slot):
        cols = slice(c * FF_CHUNK, (c + 1) * FF_CHUNK)
        up = jnp.dot(hb_ref[...], wu_ref[:, cols], preferred_element_type=F32)
        buf_ref[slot, 0:CONV_HALO, :] = halo_ref[c]
        buf_ref[slot, CONV_HALO:CONV_HALO + tm, :] = up
        halo_ref[c] = up[tm - CONV_HALO:, :]
        cw = cw_ref[:, cols]
        y = up * cw[CONV_WIDTH - 1:CONV_WIDTH, :] + cb_ref[:, cols]
        for d in range(1, CONV_WIDTH):
            y = y + buf_ref[slot, CONV_HALO - d:CONV_HALO - d + tm, :] * cw[CONV_WIDTH - 1 - d:CONV_WIDTH - d, :]
        return y

    for c in range(n_chunks):
        gate = conv(c, 2 * (c % 2))
        val = conv(c + n_chunks, 2 * (c % 2) + 1)
        act_ref[:, c * FF_CHUNK:(c + 1) * FF_CHUNK] = (gate * (1.0 / (1.0 + jnp.exp2(gate * -LOG2E))) * val).astype(BF16)
    o_ref[0] = x + jnp.dot(act_ref[...], wd_ref[...], preferred_element_type=F32)


def _conv_ffn(x, g, w_up, conv_w, conv_b, w_down, *, tm):
    B, S, D = x.shape
    d_ff = w_down.shape[0]
    n_chunks = d_ff // FF_CHUNK
    const2 = lambda b, s: (0, 0)
    tile3 = lambda b, s: (b, s, 0)
    return pl.pallas_call(
        functools.partial(_conv_ffn_kernel, tm=tm, n_chunks=n_chunks),
        grid=(B, S // tm),
        in_specs=[
            pl.BlockSpec((1, tm, D), tile3),
            pl.BlockSpec((1, D), const2),
            pl.BlockSpec(w_up.shape, const2),
            pl.BlockSpec(conv_w.shape, const2),
            pl.BlockSpec(conv_b.shape, const2),
            pl.BlockSpec(w_down.shape, const2),
        ],
        out_specs=pl.BlockSpec((1, tm, D), tile3),
        out_shape=jax.ShapeDtypeStruct((B, S, D), F32),
        scratch_shapes=[
            pltpu.VMEM((tm, D), BF16),
            pltpu.VMEM((2 * n_chunks, CONV_HALO, FF_CHUNK), F32),
            pltpu.VMEM((4, CONV_HALO + tm, FF_CHUNK), F32),
            pltpu.VMEM((tm, d_ff), BF16),
        ],
        compiler_params=pltpu.CompilerParams(
            dimension_semantics=("arbitrary", "arbitrary"), vmem_limit_bytes=VMEM_LIMIT),
        name="conv_ffn",
    )(x, g, w_up, conv_w, conv_b, w_down)


def _rope_tables(S):
    half = HEAD_DIM // 2
    inv_freq = ROPE_THETA ** (-jnp.arange(half, dtype=F32) / half)
    ang = jnp.arange(S).astype(F32)[:, None] * inv_freq[None, :]
    cos, sin = jnp.cos(ang), jnp.sin(ang)
    return (jnp.concatenate([cos, cos, cos, cos], axis=-1),
            jnp.concatenate([-sin, -sin, sin, sin], axis=-1))


def _slab_gain(g):
    lo, hi = g[:HEAD_DIM // 2], g[HEAD_DIM // 2:]
    return jnp.concatenate([lo, lo, hi, hi]).reshape(1, LANES)


def _slab_columns(w):
    rows, width = w.shape
    w = w.reshape(rows, width // LANES, 2, 2, HEAD_DIM // 2)
    return w.transpose(0, 1, 3, 2, 4).reshape(rows, width)


def kernel(x, mem, norm_mix_g, w_in, pool_w, pool_scale, q_norm_g, k_norm_g, w_out, norm_xattn_g, norm_mem_g, w_xq, w_xkv, xq_norm_g, xk_norm_g, w_xo, norm_ffn_g, w_up, conv_w, conv_b, w_down):
    B, S, D = x.shape
    depth = w_in.shape[0]
    d_ff = w_down.shape[1]
    assert all(S % t == 0 for t in (IN_PROJ_TILE, MIX_TILE, FFN_TILE, MOBA_QBLOCKS * MOBA_BLOCK)) and d_ff % FF_CHUNK == 0
    cos, sin = _rope_tables(S)
    row = lambda a: a.reshape(1, -1)
    pool_width = pool_w.shape[1] * pool_w.shape[2]
    attn_width = (w_in.shape[2] - pool_width) // 3
    q0, k0, v0 = pool_width, pool_width + attn_width, pool_width + 2 * attn_width
    for l in range(depth):
        w_in_l = jnp.concatenate([w_in[l][:, :q0], _slab_columns(w_in[l][:, q0:k0]),
                                  _slab_columns(w_in[l][:, k0:v0]), w_in[l][:, v0:]], axis=1).astype(BF16)
        pool, q, k, v, kmean = _in_proj(
            x, row(norm_mix_g[l]), w_in_l, pool_w[l].astype(BF16), row(pool_scale[l]),
            _slab_gain(q_norm_g[l]), _slab_gain(k_norm_g[l]), cos, sin, tm=IN_PROJ_TILE)
        attn = _moba(q, k, v, kmean.reshape(B, S // MOBA_BLOCK, -1), npairs=MOBA_PAIRS)
        kx, vx = _mem_kv(mem, row(norm_mem_g[l]), w_xkv[l].astype(BF16), row(xk_norm_g[l]))
        x = _mix_xattn(x, pool, attn, w_out[l].astype(BF16), row(norm_xattn_g[l]), w_xq[l].astype(BF16),
                       row(xq_norm_g[l]), kx, vx, w_xo[l].astype(BF16), tm=MIX_TILE)
        x = _conv_ffn(x, row(norm_ffn_g[l]), w_up[l].astype(BF16), conv_w[l], row(conv_b[l]), w_down[l].astype(BF16),
                      tm=FFN_TILE)
    return x
```

```python
import functools

import jax
import jax.numpy as jnp
from jax import lax
from jax.experimental import pallas as pl
from jax.experimental.pallas import tpu as pltpu

F32 = jnp.float32
BF16 = jnp.bfloat16

EPS = 1e-6
LANES = 128
SUBLANES = 8
HEAD_DIM = 64
POOL_WINDOWS = (2, 4, 8, 16)
POOL_HALO = SUBLANES * len(POOL_WINDOWS)
MOBA_BLOCK = 256
MOBA_TOPK = 3
MOBA_QBLOCKS = 4
MOBA_PAIRS = 2
ROPE_THETA = 10000.0
XATTN_HEADS = 4
CONV_WIDTH = 3
CONV_HALO = 8
FF_CHUNK = 256
NEG = -1e30
LOG2E = 1.4426950408889634
VMEM_LIMIT = 56 * 1024 * 1024
IN_PROJ_TILE = 1024
MIX_TILE = 1024
FFN_TILE = 512

NT_DIMS = (((1,), (1,)), ((), ()))
TN_DIMS = (((0,), (0,)), ((), ()))


def _rms(x, g):
    return x * lax.rsqrt(jnp.mean(x * x, axis=-1, keepdims=True) + EPS) * g


def _first_head_lanes():
    lane = lax.broadcasted_iota(jnp.int32, (1, LANES), 1)
    return (lane & (HEAD_DIM // 2)) == 0


def _head_norm_rope(y, g, cos, sin_signed):
    first = _first_head_lanes()
    sq = y * y
    s0 = jnp.sum(jnp.where(first, sq, 0.0), axis=-1, keepdims=True)
    s1 = jnp.sum(jnp.where(first, 0.0, sq), axis=-1, keepdims=True)
    r = lax.rsqrt(jnp.where(first, s0, s1) * (1.0 / HEAD_DIM) + EPS)
    yn = y * r * g
    return yn * cos + pltpu.roll(yn, LANES // 2, axis=1) * sin_signed


def _in_proj_kernel(x_ref, g_ref, w_ref, pw_ref, ps_ref, qg_ref, kg_ref, cos_ref, sin_ref,
                    pool_ref, q_ref, k_ref, v_ref, kmean_ref, ubuf_ref, lva_ref, lvb_ref, hb_ref, y_ref, vraw_ref,
                    *, tm, pool_width, attn_width):
    s = pl.program_id(1)
    half = tm // 2

    @pl.when(s == 0)
    def _():
        ubuf_ref[0:POOL_HALO, :] = jnp.zeros((POOL_HALO, pool_width), F32)

    q0, v0 = pool_width, pool_width + 2 * attn_width
    gd = pool_width // len(POOL_WINDOWS)
    cos_all, sin_all = cos_ref, sin_ref
    qg = qg_ref[...] * (HEAD_DIM ** -0.5 * LOG2E)

    def norm(h):
        rows = slice(h * half, (h + 1) * half)
        hb_ref[h] = _rms(x_ref[0, rows, :], g_ref[...]).astype(BF16)

    def project(h):
        hb = hb_ref[h]
        y_ref[h] = jnp.dot(hb, w_ref[:, q0:v0], preferred_element_type=F32)
        ubuf_ref[POOL_HALO + h * half:POOL_HALO + (h + 1) * half, :] = jnp.dot(
            hb, w_ref[:, 0:pool_width], preferred_element_type=F32)
        vraw_ref[h] = jnp.dot(hb, w_ref[:, v0:v0 + attn_width], preferred_element_type=F32)

    def finish(h):
        rows = slice(h * half, (h + 1) * half)
        cos = cos_all[rows, :]
        sin = sin_all[rows, :]
        for c in range(attn_width // LANES):
            lanes = slice(c * LANES, (c + 1) * LANES)
            q_ref[0, rows, lanes] = _head_norm_rope(y_ref[h, :, lanes], qg, cos, sin).astype(BF16)
            kr = _head_norm_rope(y_ref[h, :, attn_width + c * LANES:attn_width + (c + 1) * LANES], kg_ref[...], cos, sin)
            k_ref[0, rows, lanes] = kr.astype(BF16)
            for r in range(half // MOBA_BLOCK):
                km = jnp.sum(kr[r * MOBA_BLOCK:(r + 1) * MOBA_BLOCK], axis=0, keepdims=True) * (1.0 / MOBA_BLOCK)
                kmean_ref[0, h * (half // MOBA_BLOCK) + r, :, lanes] = km
        lo, end = h * half, POOL_HALO + (h + 1) * half
        src = ubuf_ref
        for k in range(1, len(POOL_WINDOWS) + 1):
            dst = lva_ref if k % 2 else lvb_ref
            assert POOL_WINDOWS[k - 1] == 2 ** k
            shift, r0, l0 = 2 ** (k - 1), lo + SUBLANES * k, (k - 1) * gd
            dst[r0:end, l0:] = src[r0:end, l0:] + src[r0 - shift:end - shift, l0:]
            src = dst
        t = s * tm + lo + lax.broadcasted_iota(jnp.int32, (half, 1), 0)
        for g, w in enumerate(POOL_WINDOWS):
            lanes = slice(g * gd, (g + 1) * gd)
            win = (lva_ref if (g + 1) % 2 else lvb_ref)[POOL_HALO + lo:end, lanes]
            cnt = jnp.minimum(t + 1, w).astype(F32)
            pooled = win / cnt - ubuf_ref[POOL_HALO + lo:end, lanes]
            mixed = jnp.dot(pooled.astype(BF16), pw_ref[g], preferred_element_type=F32)
            pool_ref[0, rows, lanes] = (mixed * ps_ref[:, lanes]).astype(BF16)
        lane = lax.broadcasted_iota(jnp.int32, (1, LANES), 1)
        for pr in range(attn_width // LANES):
            pair = vraw_ref[h, :, pr * LANES:(pr + 1) * LANES]
            for hh in range(2):
                keep = (lane < HEAD_DIM) if hh == 0 else (lane >= HEAD_DIM)
                ones = jnp.where(lane == _ones_lane(hh), 1.0, 0.0)
                hd = 2 * pr + hh
                v_ref[0, rows, hd * LANES:(hd + 1) * LANES] = jnp.where(keep, pair, ones).astype(BF16)

    norm(0)
    project(0)
    norm(1)
    finish(0)
    project(1)
    finish(1)
    ubuf_ref[0:POOL_HALO, :] = ubuf_ref[tm:tm + POOL_HALO, :]


def _ones_lane(hh):
    return HEAD_DIM if hh == 0 else 0


def _in_proj(x, g, w_in, pool_w, pool_scale, qg, kg, cos, sin, *, tm):
    B, S, D = x.shape
    n_groups, gd, _ = pool_w.shape
    pool_width = n_groups * gd
    attn_width = (w_in.shape[1] - pool_width) // 3
    nblk = S // MOBA_BLOCK
    kern = functools.partial(_in_proj_kernel, tm=tm, pool_width=pool_width, attn_width=attn_width)
    const2 = lambda b, s: (0, 0)
    tile3 = lambda b, s: (b, s, 0)
    return pl.pallas_call(
        kern,
        grid=(B, S // tm),
        in_specs=[
            pl.BlockSpec((1, tm, D), tile3),
            pl.BlockSpec((1, D), const2),
            pl.BlockSpec(w_in.shape, const2),
            pl.BlockSpec(pool_w.shape, lambda b, s: (0, 0, 0)),
            pl.BlockSpec((1, pool_width), const2),
            pl.BlockSpec((1, LANES), const2),
            pl.BlockSpec((1, LANES), const2),
            pl.BlockSpec((tm, LANES), lambda b, s: (s, 0)),
            pl.BlockSpec((tm, LANES), lambda b, s: (s, 0)),
        ],
        out_specs=[
            pl.BlockSpec((1, tm, pool_width), tile3),
            pl.BlockSpec((1, tm, attn_width), tile3),
            pl.BlockSpec((1, tm, attn_width), tile3),
            pl.BlockSpec((1, tm, 2 * attn_width), tile3),
            pl.BlockSpec((1, tm // MOBA_BLOCK, 1, attn_width), lambda b, s: (b, s, 0, 0)),
        ],
        out_shape=[
            jax.ShapeDtypeStruct((B, S, pool_width), BF16),
            jax.ShapeDtypeStruct((B, S, attn_width), BF16),
            jax.ShapeDtypeStruct((B, S, attn_width), BF16),
            jax.ShapeDtypeStruct((B, S, 2 * attn_width), BF16),
            jax.ShapeDtypeStruct((B, nblk, 1, attn_width), F32),
        ],
        scratch_shapes=[pltpu.VMEM((POOL_HALO + tm, pool_width), F32)] * 3 + [
            pltpu.VMEM((2, tm // 2, D), BF16), pltpu.VMEM((2, tm // 2, 2 * attn_width), F32),
            pltpu.VMEM((2, tm // 2, attn_width), F32)],
        compiler_params=pltpu.CompilerParams(
            dimension_semantics=("arbitrary", "arbitrary"), vmem_limit_bytes=VMEM_LIMIT),
        name="in_proj",
    )(x, g, w_in, pool_w, pool_scale, qg, kg, cos, sin)


def _moba_kernel(q_ref, k_ref, v_ref, kmean_ref, o_ref, qh_ref, sel_ref, s_ref, m_ref, acc_ref, *, nblk, npairs):
    mstep = pl.program_id(2)
    blk = MOBA_BLOCK
    nheads = 2 * npairs
    both = tuple(range(MOBA_QBLOCKS))
    first = _first_head_lanes()
    bidx = lax.broadcasted_iota(jnp.int32, (nblk, 1), 0)

    for qb in both:
        eligible = bidx < MOBA_QBLOCKS * mstep + qb
        for pr in range(npairs):
            lanes = slice(pr * LANES, (pr + 1) * LANES)
            qs = q_ref[0, qb * blk:(qb + 1) * blk, lanes]
            zero = jnp.zeros_like(qs)
            km = kmean_ref[0, :, lanes]
            km_hi = km.astype(BF16)
            km_split = jnp.concatenate([km_hi, (km - km_hi.astype(F32)).astype(BF16)], axis=0)
            for hh in range(2):
                st = qb * nheads + 2 * pr + hh
                qm = jnp.where(first, qs, zero) if hh == 0 else jnp.where(first, zero, qs)
                qh_ref[st] = qm.T
                split = jnp.dot(km_split, qh_ref[st], preferred_element_type=F32)
                gate = jnp.where(eligible, split[:nblk] + split[nblk:], -jnp.inf)
                picked = jnp.zeros((nblk, blk), jnp.bool_)
                for _ in range(MOBA_TOPK):
                    best = jnp.max(gate, axis=0, keepdims=True)
                    where_best = jnp.min(jnp.where(gate == best, bidx, nblk), axis=0, keepdims=True)
                    hit = bidx == where_best
                    picked = picked | hit
                    gate = jnp.where(hit, -jnp.inf, gate)
                sel = jnp.where(picked & eligible, 1.0, 0.0)
                for j in range(nblk):
                    sel_ref[st, j] = sel[j:j + 1, :]

    m_ref[...] = jnp.full(m_ref.shape, NEG, F32)
    acc_ref[...] = jnp.zeros_like(acc_ref)

    def scores(slot, kblock, qbs):
        off = pl.multiple_of(kblock * blk, blk)
        for pr in range(npairs):
            kb = k_ref[0, pl.ds(off, blk), pr * LANES:(pr + 1) * LANES]
            for qb in qbs:
                for hh in range(2):
                    st = qb * nheads + 2 * pr + hh
                    s_ref[slot, st] = jnp.dot(kb, qh_ref[st], preferred_element_type=F32)

    def accumulate(slot, kblock, past, qbs):
        off = pl.multiple_of(kblock * blk, blk)
        for qb in qbs:
            for pr in range(npairs):
                for hh in range(2):
                    h = 2 * pr + hh
                    st = qb * nheads + h
                    vb = v_ref[0, pl.ds(off, blk), h * LANES:(h + 1) * LANES]
                    sc = s_ref[slot, st]
                    m = m_ref[st]
                    if not past:
                        causal = (lax.broadcasted_iota(jnp.int32, (blk, blk), 0)
                                  <= lax.broadcasted_iota(jnp.int32, (blk, blk), 1))
                        sc = jnp.where(causal, sc, NEG)
                        m_new = jnp.maximum(m, jnp.max(sc, axis=0, keepdims=True))
                        p = jnp.exp2(sc - m_new)
                    else:
                        picked = sel_ref[st, kblock] > 0.0
                        m_new = jnp.maximum(m, jnp.where(picked, jnp.max(sc, axis=0, keepdims=True), NEG))
                        p = jnp.exp2(sc - jnp.where(picked, m_new, -NEG))
                    alpha = jnp.exp2(m - m_new)
                    m_ref[st] = m_new
                    pv = lax.dot_general(vb, p.astype(BF16), TN_DIMS, preferred_element_type=F32)
                    ones_row = _ones_lane(hh)
                    acc_ref[st, 0:HEAD_DIM] = alpha * acc_ref[st, 0:HEAD_DIM] + pv[hh * HEAD_DIM:(hh + 1) * HEAD_DIM]
                    acc_ref[st, HEAD_DIM:] = alpha * acc_ref[st, HEAD_DIM:] + pv[ones_row:ones_row + SUBLANES]

    scores(0, 0, both)

    def quad(t, _):
        for a in (4 * t, 4 * t + 2):
            scores(1, a + 1, both)
            accumulate(0, a, True, both)
            scores(0, a + 2, both)
            accumulate(1, a + 1, True, both)
        return 0

    lax.fori_loop(0, (MOBA_QBLOCKS // 4) * mstep, quad, 0)
    base = MOBA_QBLOCKS * mstep
    for r in range(MOBA_QBLOCKS):
        later = tuple(range(r + 1, MOBA_QBLOCKS))
        if later:
            scores((r + 1) % 2, base + r + 1, later)
        accumulate(r % 2, base + r, False, (r,))
        if later:
            accumulate(r % 2, base + r, True, later)

    for qb in both:
        for pr in range(npairs):
            sts = (qb * nheads + 2 * pr, qb * nheads + 2 * pr + 1)
            halves = [acc_ref[st, 0:HEAD_DIM] / acc_ref[st, HEAD_DIM:HEAD_DIM + 1] for st in sts]
            o_ref[0, qb * blk:(qb + 1) * blk, pr * LANES:(pr + 1) * LANES] = (
                jnp.concatenate(halves, axis=0).T.astype(BF16))


def _moba(q, k, v, kmean, *, npairs):
    B, S, W = q.shape
    nblk = S // MOBA_BLOCK
    wstep = npairs * LANES
    nstreams = MOBA_QBLOCKS * 2 * npairs
    kern = functools.partial(_moba_kernel, nblk=nblk, npairs=npairs)
    return pl.pallas_call(
        kern,
        grid=(B, W // wstep, nblk // MOBA_QBLOCKS),
        in_specs=[
            pl.BlockSpec((1, MOBA_QBLOCKS * MOBA_BLOCK, wstep), lambda b, g, i: (b, i, g)),
            pl.BlockSpec((1, S, wstep), lambda b, g, i: (b, 0, g)),
            pl.BlockSpec((1, S, 2 * wstep), lambda b, g, i: (b, 0, g)),
            pl.BlockSpec((1, nblk, wstep), lambda b, g, i: (b, 0, g)),
        ],
        out_specs=pl.BlockSpec((1, MOBA_QBLOCKS * MOBA_BLOCK, wstep), lambda b, g, i: (b, i, g)),
        out_shape=jax.ShapeDtypeStruct((B, S, W), BF16),
        scratch_shapes=[pltpu.VMEM((nstreams, LANES, MOBA_BLOCK), BF16),
                        pltpu.VMEM((nstreams, nblk, 1, MOBA_BLOCK), F32),
                        pltpu.VMEM((2, nstreams, MOBA_BLOCK, MOBA_BLOCK), F32),
                        pltpu.VMEM((nstreams, 1, MOBA_BLOCK), F32),
                        pltpu.VMEM((nstreams, HEAD_DIM + SUBLANES, MOBA_BLOCK), F32)],
        compiler_params=pltpu.CompilerParams(
            dimension_semantics=("arbitrary", "arbitrary", "arbitrary"), vmem_limit_bytes=VMEM_LIMIT),
        name="moba",
    )(q, k, v, kmean)


def _mem_kv_kernel(mem_ref, g_ref, w_ref, kg_ref, k_ref, v_ref, *, d_model):
    mb = _rms(mem_ref[0], g_ref[...]).astype(BF16)
    hd = d_model // XATTN_HEADS
    for h in range(XATTN_HEADS):
        lanes = slice(h * hd, (h + 1) * hd)
        kh = jnp.dot(mb, w_ref[:, lanes], preferred_element_type=F32)
        k_ref[0, :, lanes] = _rms(kh, kg_ref[...]).astype(BF16)
    v_ref[0] = jnp.dot(mb, w_ref[:, d_model:2 * d_model], preferred_element_type=F32).astype(BF16)


def _mem_kv(mem, g, w_xkv, kg):
    B, M, D = mem.shape
    const2 = lambda b: (0, 0)
    blk3 = lambda b: (b, 0, 0)
    return pl.pallas_call(
        functools.partial(_mem_kv_kernel, d_model=D),
        grid=(B,),
        in_specs=[pl.BlockSpec((1, M, D), blk3), pl.BlockSpec((1, D), const2),
                  pl.BlockSpec(w_xkv.shape, const2), pl.BlockSpec((1, D // XATTN_HEADS), const2)],
        out_specs=[pl.BlockSpec((1, M, D), blk3), pl.BlockSpec((1, M, D), blk3)],
        out_shape=[jax.ShapeDtypeStruct((B, M, D), BF16)] * 2,
        compiler_params=pltpu.CompilerParams(
            dimension_semantics=("arbitrary",), vmem_limit_bytes=VMEM_LIMIT),
        name="mem_kv",
    )(mem, g, w_xkv, kg)


def _mix_xattn_kernel(x_ref, pool_ref, attn_ref, wo_ref, g_ref, wq_ref, qg_ref, k_ref, v_ref, wxo_ref,
                      o_ref, q_ref, qn_ref, s_ref, p_ref, oh_ref, *, pool_width, d_model):
    x1 = (x_ref[0]
          + jnp.dot(pool_ref[0], wo_ref[0:pool_width, :], preferred_element_type=F32)
          + jnp.dot(attn_ref[0], wo_ref[pool_width:, :], preferred_element_type=F32))
    hb = _rms(x1, g_ref[...]).astype(BF16)
    hd = d_model // XATTN_HEADS
    heads = [slice(h * hd, (h + 1) * hd) for h in range(XATTN_HEADS)]
    q_ref[...] = jnp.dot(hb, wq_ref[...], preferred_element_type=F32)
    for lanes in heads:
        qn_ref[:, lanes] = (_rms(q_ref[:, lanes], qg_ref[...]) * (hd ** -0.5)).astype(BF16)
    for h, lanes in enumerate(heads):
        s_ref[h] = lax.dot_general(qn_ref[:, lanes], k_ref[0, :, lanes], NT_DIMS,
                                   preferred_element_type=F32)
    for h in range(XATTN_HEADS):
        s = s_ref[h]
        p = jnp.exp(s - jnp.max(s, axis=-1, keepdims=True))
        p_ref[h] = (p / jnp.sum(p, axis=-1, keepdims=True)).astype(BF16)
    for h, lanes in enumerate(heads):
        oh_ref[:, lanes] = jnp.dot(p_ref[h], v_ref[0, :, lanes], preferred_element_type=F32).astype(BF16)
    o_ref[0] = x1 + jnp.dot(oh_ref[...], wxo_ref[...], preferred_element_type=F32)


def _mix_xattn(x, pool, attn, w_out, g, w_xq, qg, kx, vx, w_xo, *, tm):
    B, S, D = x.shape
    M = kx.shape[1]
    pool_width = pool.shape[2]
    const2 = lambda b, s: (0, 0)
    tile3 = lambda b, s: (b, s, 0)
    mem3 = lambda b, s: (b, 0, 0)
    return pl.pallas_call(
        functools.partial(_mix_xattn_kernel, pool_width=pool_width, d_model=D),
        grid=(B, S // tm),
        in_specs=[
            pl.BlockSpec((1, tm, D), tile3),
            pl.BlockSpec((1, tm, pool_width), tile3),
            pl.BlockSpec((1, tm, attn.shape[2]), tile3),
            pl.BlockSpec(w_out.shape, const2),
            pl.BlockSpec((1, D), const2),
            pl.BlockSpec(w_xq.shape, const2),
            pl.BlockSpec((1, D // XATTN_HEADS), const2),
            pl.BlockSpec((1, M, D), mem3),
            pl.BlockSpec((1, M, D), mem3),
            pl.BlockSpec(w_xo.shape, const2),
        ],
        out_specs=pl.BlockSpec((1, tm, D), tile3),
        out_shape=jax.ShapeDtypeStruct((B, S, D), F32),
        scratch_shapes=[pltpu.VMEM((tm, D), F32), pltpu.VMEM((tm, D), BF16),
                        pltpu.VMEM((XATTN_HEADS, tm, M), F32), pltpu.VMEM((XATTN_HEADS, tm, M), BF16),
                        pltpu.VMEM((tm, D), BF16)],
        compiler_params=pltpu.CompilerParams(
            dimension_semantics=("arbitrary", "arbitrary"), vmem_limit_bytes=VMEM_LIMIT),
        name="mix_xattn",
    )(x, pool, attn, w_out, g, w_xq, qg, kx, vx, w_xo)


def _conv_ffn_kernel(x_ref, g_ref, wu_ref, cw_ref, cb_ref, wd_ref, o_ref,
                     hb_ref, halo_ref, buf_ref, act_ref, *, tm, n_chunks):
    s = pl.program_id(1)

    @pl.when(s == 0)
    def _():
        halo_ref[...] = jnp.zeros_like(halo_ref)

    x = x_ref[0]
    hb_ref[...] = _rms(x, g_ref[...]).astype(BF16)

    def conv(c, slot):
        cols = slice(c * FF_CHUNK, (c + 1) * FF_CHUNK)
        up = jnp.dot(hb_ref[...], wu_ref[:, cols], preferred_element_type=F32)
        buf_ref[slot, 0:CONV_HALO, :] = halo_ref[c]
        buf_ref[slot, CONV_HALO:CONV_HALO + tm, :] = up
        halo_ref[c] = up[tm - CONV_HALO:, :]
        cw = cw_ref[:, cols]
        y = up * cw[CONV_WIDTH - 1:CONV_WIDTH, :] + cb_ref[:, cols]
        for d in range(1, CONV_WIDTH):
            y = y + buf_ref[slot, CONV_HALO - d:CONV_HALO - d + tm, :] * cw[CONV_WIDTH - 1 - d:CONV_WIDTH - d, :]
        return y

    for c in range(n_chunks):
        gate = conv(c, 2 * (c % 2))
        val = conv(c + n_chunks, 2 * (c % 2) + 1)
        act_ref[:, c * FF_CHUNK:(c + 1) * FF_CHUNK] = (gate * (1.0 / (1.0 + jnp.exp2(gate * -LOG2E))) * val).astype(BF16)
    o_ref[0] = x + jnp.dot(act_ref[...], wd_ref[...], preferred_element_type=F32)


def _conv_ffn(x, g, w_up, conv_w, conv_b, w_down, *, tm):
    B, S, D = x.shape
    d_ff = w_down.shape[0]
    n_chunks = d_ff // FF_CHUNK
    const2 = lambda b, s: (0, 0)
    tile3 = lambda b, s: (b, s, 0)
    return pl.pallas_call(
        functools.partial(_conv_ffn_kernel, tm=tm, n_chunks=n_chunks),
        grid=(B, S // tm),
        in_specs=[
            pl.BlockSpec((1, tm, D), tile3),
            pl.BlockSpec((1, D), const2),
            pl.BlockSpec(w_up.shape, const2),
            pl.BlockSpec(conv_w.shape, const2),
            pl.BlockSpec(conv_b.shape, const2),
            pl.BlockSpec(w_down.shape, const2),
        ],
        out_specs=pl.BlockSpec((1, tm, D), tile3),
        out_shape=jax.ShapeDtypeStruct((B, S, D), F32),
        scratch_shapes=[
            pltpu.VMEM((tm, D), BF16),
            pltpu.VMEM((2 * n_chunks, CONV_HALO, FF_CHUNK), F32),
            pltpu.VMEM((4, CONV_HALO + tm, FF_CHUNK), F32),
            pltpu.VMEM((tm, d_ff), BF16),
        ],
        compiler_params=pltpu.CompilerParams(
            dimension_semantics=("arbitrary", "arbitrary"), vmem_limit_bytes=VMEM_LIMIT),
        name="conv_ffn",
    )(x, g, w_up, conv_w, conv_b, w_down)


def _rope_tables(S):
    half = HEAD_DIM // 2
    inv_freq = ROPE_THETA ** (-jnp.arange(half, dtype=F32) / half)
    ang = jnp.arange(S).astype(F32)[:, None] * inv_freq[None, :]
    cos, sin = jnp.cos(ang), jnp.sin(ang)
    return (jnp.concatenate([cos, cos, cos, cos], axis=-1),
            jnp.concatenate([-sin, -sin, sin, sin], axis=-1))


def _slab_gain(g):
    lo, hi = g[:HEAD_DIM // 2], g[HEAD_DIM // 2:]
    return jnp.concatenate([lo, lo, hi, hi]).reshape(1, LANES)


def _slab_columns(w):
    rows, width = w.shape
    w = w.reshape(rows, width // LANES, 2, 2, HEAD_DIM // 2)
    return w.transpose(0, 1, 3, 2, 4).reshape(rows, width)


def kernel(x, mem, norm_mix_g, w_in, pool_w, pool_scale, q_norm_g, k_norm_g, w_out, norm_xattn_g, norm_mem_g, w_xq, w_xkv, xq_norm_g, xk_norm_g, w_xo, norm_ffn_g, w_up, conv_w, conv_b, w_down):
    B, S, D = x.shape
    depth = w_in.shape[0]
    d_ff = w_down.shape[1]
    assert all(S % t == 0 for t in (IN_PROJ_TILE, MIX_TILE, FFN_TILE, MOBA_QBLOCKS * MOBA_BLOCK)) and d_ff % FF_CHUNK == 0
    cos, sin = _rope_tables(S)
    row = lambda a: a.reshape(1, -1)
    pool_width = pool_w.shape[1] * pool_w.shape[2]
    attn_width = (w_in.shape[2] - pool_width) // 3
    q0, k0, v0 = pool_width, pool_width + attn_width, pool_width + 2 * attn_width
    for l in range(depth):
        w_in_l = jnp.concatenate([w_in[l][:, :q0], _slab_columns(w_in[l][:, q0:k0]),
                                  _slab_columns(w_in[l][:, k0:v0]), w_in[l][:, v0:]], axis=1).astype(BF16)
        pool, q, k, v, kmean = _in_proj(
            x, row(norm_mix_g[l]), w_in_l, pool_w[l].astype(BF16), row(pool_scale[l]),
            _slab_gain(q_norm_g[l]), _slab_gain(k_norm_g[l]), cos, sin, tm=IN_PROJ_TILE)
        attn = _moba(q, k, v, kmean.reshape(B, S // MOBA_BLOCK, -1), npairs=MOBA_PAIRS)
        kx, vx = _mem_kv(mem, row(norm_mem_g[l]), w_xkv[l].astype(BF16), row(xk_norm_g[l]))
        x = _mix_xattn(x, pool, attn, w_out[l].astype(BF16), row(norm_xattn_g[l]), w_xq[l].astype(BF16),
                       row(xq_norm_g[l]), kx, vx, w_xo[l].astype(BF16), tm=MIX_TILE)
        x = _conv_ffn(x, row(norm_ffn_g[l]), w_up[l].astype(BF16), conv_w[l], row(conv_b[l]), w_down[l].astype(BF16),
                      tm=FFN_TILE)
    return x
```

```python
import functools

import jax
import jax.numpy as jnp
from jax import lax
from jax.experimental import pallas as pl
from jax.experimental.pallas import tpu as pltpu

F32 = jnp.float32
BF16 = jnp.bfloat16

EPS = 1e-6
LANES = 128
SUBLANES = 8
HEAD_DIM = 64
POOL_WINDOWS = (2, 4, 8, 16)
POOL_HALO = SUBLANES * len(POOL_WINDOWS)
MOBA_BLOCK = 256
MOBA_TOPK = 3
MOBA_QBLOCKS = 4
MOBA_PAIRS = 2
ROPE_THETA = 10000.0
XATTN_HEADS = 4
CONV_WIDTH = 3
CONV_HALO = 8
FF_CHUNK = 256
NEG = -1e30
LOG2E = 1.4426950408889634
VMEM_LIMIT = 56 * 1024 * 1024
IN_PROJ_TILE = 1024
MIX_TILE = 1024
FFN_TILE = 512

NT_DIMS = (((1,), (1,)), ((), ()))
TN_DIMS = (((0,), (0,)), ((), ()))


def _rms(x, g):
    return x * lax.rsqrt(jnp.mean(x * x, axis=-1, keepdims=True) + EPS) * g


def _first_head_lanes():
    lane = lax.broadcasted_iota(jnp.int32, (1, LANES), 1)
    return (lane & (HEAD_DIM // 2)) == 0


def _head_norm_rope(y, g, cos, sin_signed):
    first = _first_head_lanes()
    sq = y * y
    s0 = jnp.sum(jnp.where(first, sq, 0.0), axis=-1, keepdims=True)
    s1 = jnp.sum(jnp.where(first, 0.0, sq), axis=-1, keepdims=True)
    r = lax.rsqrt(jnp.where(first, s0, s1) * (1.0 / HEAD_DIM) + EPS)
    yn = y * r * g
    return yn * cos + pltpu.roll(yn, LANES // 2, axis=1) * sin_signed


def _in_proj_kernel(x_ref, g_ref, w_ref, pw_ref, ps_ref, qg_ref, kg_ref, cos_ref, sin_ref,
                    pool_ref, q_ref, k_ref, v_ref, kmean_ref, ubuf_ref, lva_ref, lvb_ref, hb_ref, y_ref, vraw_ref,
                    *, tm, pool_width, attn_width):
    s = pl.program_id(1)
    half = tm // 2

    @pl.when(s == 0)
    def _():
        ubuf_ref[0:POOL_HALO, :] = jnp.zeros((POOL_HALO, pool_width), F32)

    q0, v0 = pool_width, pool_width + 2 * attn_width
    gd = pool_width // len(POOL_WINDOWS)
    cos_all, sin_all = cos_ref, sin_ref
    qg = qg_ref[...] * (HEAD_DIM ** -0.5 * LOG2E)

    def norm(h):
        rows = slice(h * half, (h + 1) * half)
        hb_ref[h] = _rms(x_ref[0, rows, :], g_ref[...]).astype(BF16)

    def project(h):
        hb = hb_ref[h]
        y_ref[h] = jnp.dot(hb, w_ref[:, q0:v0], preferred_element_type=F32)
        ubuf_ref[POOL_HALO + h * half:POOL_HALO + (h + 1) * half, :] = jnp.dot(
            hb, w_ref[:, 0:pool_width], preferred_element_type=F32)
        vraw_ref[h] = jnp.dot(hb, w_ref[:, v0:v0 + attn_width], preferred_element_type=F32)

    def finish(h):
        rows = slice(h * half, (h + 1) * half)
        cos = cos_all[rows, :]
        sin = sin_all[rows, :]
        for c in range(attn_width // LANES):
            lanes = slice(c * LANES, (c + 1) * LANES)
            q_ref[0, rows, lanes] = _head_norm_rope(y_ref[h, :, lanes], qg, cos, sin).astype(BF16)
            kr = _head_norm_rope(y_ref[h, :, attn_width + c * LANES:attn_width + (c + 1) * LANES], kg_ref[...], cos, sin)
            k_ref[0, rows, lanes] = kr.astype(BF16)
            for r in range(half // MOBA_BLOCK):
                km = jnp.sum(kr[r * MOBA_BLOCK:(r + 1) * MOBA_BLOCK], axis=0, keepdims=True) * (1.0 / MOBA_BLOCK)
                kmean_ref[0, h * (half // MOBA_BLOCK) + r, :, lanes] = km
        lo, end = h * half, POOL_HALO + (h + 1) * half
        src = ubuf_ref
        for k in range(1, len(POOL_WINDOWS) + 1):
            dst = lva_ref if k % 2 else lvb_ref
            assert POOL_WINDOWS[k - 1] == 2 ** k
            shift, r0, l0 = 2 ** (k - 1), lo + SUBLANES * k, (k - 1) * gd
            dst[r0:end, l0:] = src[r0:end, l0:] + src[r0 - shift:end - shift, l0:]
            src = dst
        t = s * tm + lo + lax.broadcasted_iota(jnp.int32, (half, 1), 0)
        for g, w in enumerate(POOL_WINDOWS):
            lanes = slice(g * gd, (g + 1) * gd)
            win = (lva_ref if (g + 1) % 2 else lvb_ref)[POOL_HALO + lo:end, lanes]
            cnt = jnp.minimum(t + 1, w).astype(F32)
            pooled = win / cnt - ubuf_ref[POOL_HALO + lo:end, lanes]
            mixed = jnp.dot(pooled.astype(BF16), pw_ref[g], preferred_element_type=F32)
            pool_ref[0, rows, lanes] = (mixed * ps_ref[:, lanes]).astype(BF16)
        lane = lax.broadcasted_iota(jnp.int32, (1, LANES), 1)
        for pr in range(attn_width // LANES):
            pair = vraw_ref[h, :, pr * LANES:(pr + 1) * LANES]
            for hh in range(2):
                keep = (lane < HEAD_DIM) if hh == 0 else (lane >= HEAD_DIM)
                ones = jnp.where(lane == _ones_lane(hh), 1.0, 0.0)
                hd = 2 * pr + hh
                v_ref[0, rows, hd * LANES:(hd + 1) * LANES] = jnp.where(keep, pair, ones).astype(BF16)

    norm(0)
    project(0)
    norm(1)
    finish(0)
    project(1)
    finish(1)
    ubuf_ref[0:POOL_HALO, :] = ubuf_ref[tm:tm + POOL_HALO, :]


def _ones_lane(hh):
    return HEAD_DIM if hh == 0 else 0


def _in_proj(x, g, w_in, pool_w, pool_scale, qg, kg, cos, sin, *, tm):
    B, S, D = x.shape
    n_groups, gd, _ = pool_w.shape
    pool_width = n_groups * gd
    attn_width = (w_in.shape[1] - pool_width) // 3
    nblk = S // MOBA_BLOCK
    kern = functools.partial(_in_proj_kernel, tm=tm, pool_width=pool_width, attn_width=attn_width)
    const2 = lambda b, s: (0, 0)
    tile3 = lambda b, s: (b, s, 0)
    return pl.pallas_call(
        kern,
        grid=(B, S // tm),
        in_specs=[
            pl.BlockSpec((1, tm, D), tile3),
            pl.BlockSpec((1, D), const2),
            pl.BlockSpec(w_in.shape, const2),
            pl.BlockSpec(pool_w.shape, lambda b, s: (0, 0, 0)),
            pl.BlockSpec((1, pool_width), const2),
            pl.BlockSpec((1, LANES), const2),
            pl.BlockSpec((1, LANES), const2),
            pl.BlockSpec((tm, LANES), lambda b, s: (s, 0)),
            pl.BlockSpec((tm, LANES), lambda b, s: (s, 0)),
        ],
        out_specs=[
            pl.BlockSpec((1, tm, pool_width), tile3),
            pl.BlockSpec((1, tm, attn_width), tile3),
            pl.BlockSpec((1, tm, attn_width), tile3),
            pl.BlockSpec((1, tm, 2 * attn_width), tile3),
            pl.BlockSpec((1, tm // MOBA_BLOCK, 1, attn_width), lambda b, s: (b, s, 0, 0)),
        ],
        out_shape=[
            jax.ShapeDtypeStruct((B, S, pool_width), BF16),
            jax.ShapeDtypeStruct((B, S, attn_width), BF16),
            jax.ShapeDtypeStruct((B, S, attn_width), BF16),
            jax.ShapeDtypeStruct((B, S, 2 * attn_width), BF16),
            jax.ShapeDtypeStruct((B, nblk, 1, attn_width), F32),
        ],
        scratch_shapes=[pltpu.VMEM((POOL_HALO + tm, pool_width), F32)] * 3 + [
            pltpu.VMEM((2, tm // 2, D), BF16), pltpu.VMEM((2, tm // 2, 2 * attn_width), F32),
            pltpu.VMEM((2, tm // 2, attn_width), F32)],
        compiler_params=pltpu.CompilerParams(
            dimension_semantics=("arbitrary", "arbitrary"), vmem_limit_bytes=VMEM_LIMIT),
        name="in_proj",
    )(x, g, w_in, pool_w, pool_scale, qg, kg, cos, sin)


def _moba_kernel(q_ref, k_ref, v_ref, kmean_ref, o_ref, qh_ref, sel_ref, s_ref, m_ref, acc_ref, *, nblk, npairs):
    mstep = pl.program_id(2)
    blk = MOBA_BLOCK
    nheads = 2 * npairs
    both = tuple(range(MOBA_QBLOCKS))
    first = _first_head_lanes()
    bidx = lax.broadcasted_iota(jnp.int32, (nblk, 1), 0)

    for qb in both:
        eligible = bidx < MOBA_QBLOCKS * mstep + qb
        for pr in range(npairs):
            lanes = slice(pr * LANES, (pr + 1) * LANES)
            qs = q_ref[0, qb * blk:(qb + 1) * blk, lanes]
            zero = jnp.zeros_like(qs)
            km = kmean_ref[0, :, lanes]
            km_hi = km.astype(BF16)
            km_split = jnp.concatenate([km_hi, (km - km_hi.astype(F32)).astype(BF16)], axis=0)
            for hh in range(2):
                st = qb * nheads + 2 * pr + hh
                qm = jnp.where(first, qs, zero) if hh == 0 else jnp.where(first, zero, qs)
                qh_ref[st] = qm.T
                split = jnp.dot(km_split, qh_ref[st], preferred_element_type=F32)
                gate = jnp.where(eligible, split[:nblk] + split[nblk:], -jnp.inf)
                picked = jnp.zeros((nblk, blk), jnp.bool_)
                for _ in range(MOBA_TOPK):
                    best = jnp.max(gate, axis=0, keepdims=True)
                    where_best = jnp.min(jnp.where(gate == best, bidx, nblk), axis=0, keepdims=True)
                    hit = bidx == where_best
                    picked = picked | hit
                    gate = jnp.where(hit, -jnp.inf, gate)
                sel = jnp.where(picked & eligible, 1.0, 0.0)
                for j in range(nblk):
                    sel_ref[st, j] = sel[j:j + 1, :]

    m_ref[...] = jnp.full(m_ref.shape, NEG, F32)
    acc_ref[...] = jnp.zeros_like(acc_ref)

    def scores(slot, kblock, qbs):
        off = pl.multiple_of(kblock * blk, blk)
        for pr in range(npairs):
            kb = k_ref[0, pl.ds(off, blk), pr * LANES:(pr + 1) * LANES]
            for qb in qbs:
                for hh in range(2):
                    st = qb * nheads + 2 * pr + hh
                    s_ref[slot, st] = jnp.dot(kb, qh_ref[st], preferred_element_type=F32)

    def accumulate(slot, kblock, past, qbs):
        off = pl.multiple_of(kblock * blk, blk)
        for qb in qbs:
            for pr in range(npairs):
                for hh in range(2):
                    h = 2 * pr + hh
                    st = qb * nheads + h
                    vb = v_ref[0, pl.ds(off, blk), h * LANES:(h + 1) * LANES]
                    sc = s_ref[slot, st]
                    m = m_ref[st]
                    if not past:
                        causal = (lax.broadcasted_iota(jnp.int32, (blk, blk), 0)
                                  <= lax.broadcasted_iota(jnp.int32, (blk, blk), 1))
                        sc = jnp.where(causal, sc, NEG)
                        m_new = jnp.maximum(m, jnp.max(sc, axis=0, keepdims=True))
                        p = jnp.exp2(sc - m_new)
                    else:
                        picked = sel_ref[st, kblock] > 0.0
                        m_new = jnp.maximum(m, jnp.where(picked, jnp.max(sc, axis=0, keepdims=True), NEG))
                        p = jnp.exp2(sc - jnp.where(picked, m_new, -NEG))
                    alpha = jnp.exp2(m - m_new)
                    m_ref[st] = m_new
                    pv = lax.dot_general(vb, p.astype(BF16), TN_DIMS, preferred_element_type=F32)
                    ones_row = _ones_lane(hh)
                    acc_ref[st, 0:HEAD_DIM] = alpha * acc_ref[st, 0:HEAD_DIM] + pv[hh * HEAD_DIM:(hh + 1) * HEAD_DIM]
                    acc_ref[st, HEAD_DIM:] = alpha * acc_ref[st, HEAD_DIM:] + pv[ones_row:ones_row + SUBLANES]

    scores(0, 0, both)

    def quad(t, _):
        for a in (4 * t, 4 * t + 2):
            scores(1, a + 1, both)
            accumulate(0, a, True, both)
            scores(0, a + 2, both)
            accumulate(1, a + 1, True, both)
        return 0

    lax.fori_loop(0, (MOBA_QBLOCKS // 4) * mstep, quad, 0)
    base = MOBA_QBLOCKS * mstep
    for r in range(MOBA_QBLOCKS):
        later = tuple(range(r + 1, MOBA_QBLOCKS))
        if later:
            scores((r + 1) % 2, base + r + 1, later)
        accumulate(r % 2, base + r, False, (r,))
        if later:
            accumulate(r % 2, base + r, True, later)

    for qb in both:
        for pr in range(npairs):
            sts = (qb * nheads + 2 * pr, qb * nheads + 2 * pr + 1)
            halves = [acc_ref[st, 0:HEAD_DIM] / acc_ref[st, HEAD_DIM:HEAD_DIM + 1] for st in sts]
            o_ref[0, qb * blk:(qb + 1) * blk, pr * LANES:(pr + 1) * LANES] = (
                jnp.concatenate(halves, axis=0).T.astype(BF16))


def _moba(q, k, v, kmean, *, npairs):
    B, S, W = q.shape
    nblk = S // MOBA_BLOCK
    wstep = npairs * LANES
    nstreams = MOBA_QBLOCKS * 2 * npairs
    kern = functools.partial(_moba_kernel, nblk=nblk, npairs=npairs)
    return pl.pallas_call(
        kern,
        grid=(B, W // wstep, nblk // MOBA_QBLOCKS),
        in_specs=[
            pl.BlockSpec((1, MOBA_QBLOCKS * MOBA_BLOCK, wstep), lambda b, g, i: (b, i, g)),
            pl.BlockSpec((1, S, wstep), lambda b, g, i: (b, 0, g)),
            pl.BlockSpec((1, S, 2 * wstep), lambda b, g, i: (b, 0, g)),
            pl.BlockSpec((1, nblk, wstep), lambda b, g, i: (b, 0, g)),
        ],
        out_specs=pl.BlockSpec((1, MOBA_QBLOCKS * MOBA_BLOCK, wstep), lambda b, g, i: (b, i, g)),
        out_shape=jax.ShapeDtypeStruct((B, S, W), BF16),
        scratch_shapes=[pltpu.VMEM((nstreams, LANES, MOBA_BLOCK), BF16),
                        pltpu.VMEM((nstreams, nblk, 1, MOBA_BLOCK), F32),
                        pltpu.VMEM((2, nstreams, MOBA_BLOCK, MOBA_BLOCK), F32),
                        pltpu.VMEM((nstreams, 1, MOBA_BLOCK), F32),
                        pltpu.VMEM((nstreams, HEAD_DIM + SUBLANES, MOBA_BLOCK), F32)],
        compiler_params=pltpu.CompilerParams(
            dimension_semantics=("arbitrary", "arbitrary", "arbitrary"), vmem_limit_bytes=VMEM_LIMIT),
        name="moba",
    )(q, k, v, kmean)


def _mem_kv_kernel(mem_ref, g_ref, w_ref, kg_ref, k_ref, v_ref, *, d_model):
    mb = _rms(mem_ref[0], g_ref[...]).astype(BF16)
    hd = d_model // XATTN_HEADS
    for h in range(XATTN_HEADS):
        lanes = slice(h * hd, (h + 1) * hd)
        kh = jnp.dot(mb, w_ref[:, lanes], preferred_element_type=F32)
        k_ref[0, :, lanes] = _rms(kh, kg_ref[...]).astype(BF16)
    v_ref[0] = jnp.dot(mb, w_ref[:, d_model:2 * d_model], preferred_element_type=F32).astype(BF16)


def _mem_kv(mem, g, w_xkv, kg):
    B, M, D = mem.shape
    const2 = lambda b: (0, 0)
    blk3 = lambda b: (b, 0, 0)
    return pl.pallas_call(
        functools.partial(_mem_kv_kernel, d_model=D),
        grid=(B,),
        in_specs=[pl.BlockSpec((1, M, D), blk3), pl.BlockSpec((1, D), const2),
                  pl.BlockSpec(w_xkv.shape, const2), pl.BlockSpec((1, D // XATTN_HEADS), const2)],
        out_specs=[pl.BlockSpec((1, M, D), blk3), pl.BlockSpec((1, M, D), blk3)],
        out_shape=[jax.ShapeDtypeStruct((B, M, D), BF16)] * 2,
        compiler_params=pltpu.CompilerParams(
            dimension_semantics=("arbitrary",), vmem_limit_bytes=VMEM_LIMIT),
        name="mem_kv",
    )(mem, g, w_xkv, kg)


def _mix_xattn_kernel(x_ref, pool_ref, attn_ref, wo_ref, g_ref, wq_ref, qg_ref, k_ref, v_ref, wxo_ref,
                      o_ref, x1_ref, q_ref, qn_ref, s_ref, p_ref, oh_ref, *, pool_width, d_model):
    hd = d_model // XATTN_HEADS
    heads = [slice(h * hd, (h + 1) * hd) for h in range(XATTN_HEADS)]
    tm = x_ref.shape[1]
    halves = [slice(r * (tm // 2), (r + 1) * (tm // 2)) for r in range(2)]
    for rows in halves:
        x1_ref[rows, :] = (x_ref[0, rows, :]
                           + jnp.dot(pool_ref[0, rows, :], wo_ref[0:pool_width, :], preferred_element_type=F32)
                           + jnp.dot(attn_ref[0, rows, :], wo_ref[pool_width:, :], preferred_element_type=F32))
    for rows in halves:
        hb = _rms(x1_ref[rows, :], g_ref[...]).astype(BF16)
        q_ref[rows, :] = jnp.dot(hb, wq_ref[...], preferred_element_type=F32)
    for rows in halves:
        for lanes in heads:
            qn_ref[rows, lanes] = (_rms(q_ref[rows, lanes], qg_ref[...]) * (hd ** -0.5)).astype(BF16)
        for h, lanes in enumerate(heads):
            s_ref[h, rows, :] = lax.dot_general(qn_ref[rows, lanes], k_ref[0, :, lanes], NT_DIMS,
                                                preferred_element_type=F32)
    for rows in halves:
        for h in range(XATTN_HEADS):
            s = s_ref[h, rows, :]
            p = jnp.exp(s - jnp.max(s, axis=-1, keepdims=True))
            p_ref[h, rows, :] = (p / jnp.sum(p, axis=-1, keepdims=True)).astype(BF16)
        for h, lanes in enumerate(heads):
            oh_ref[rows, lanes] = jnp.dot(p_ref[h, rows, :], v_ref[0, :, lanes],
                                          preferred_element_type=F32).astype(BF16)
    for rows in halves:
        o_ref[0, rows, :] = x1_ref[rows, :] + jnp.dot(oh_ref[rows, :], wxo_ref[...], preferred_element_type=F32)


def _mix_xattn(x, pool, attn, w_out, g, w_xq, qg, kx, vx, w_xo, *, tm):
    B, S, D = x.shape
    M = kx.shape[1]
    pool_width = pool.shape[2]
    const2 = lambda b, s: (0, 0)
    tile3 = lambda b, s: (b, s, 0)
    mem3 = lambda b, s: (b, 0, 0)
    return pl.pallas_call(
        functools.partial(_mix_xattn_kernel, pool_width=pool_width, d_model=D),
        grid=(B, S // tm),
        in_specs=[
            pl.BlockSpec((1, tm, D), tile3),
            pl.BlockSpec((1, tm, pool_width), tile3),
            pl.BlockSpec((1, tm, attn.shape[2]), tile3),
            pl.BlockSpec(w_out.shape, const2),
            pl.BlockSpec((1, D), const2),
            pl.BlockSpec(w_xq.shape, const2),
            pl.BlockSpec((1, D // XATTN_HEADS), const2),
            pl.BlockSpec((1, M, D), mem3),
            pl.BlockSpec((1, M, D), mem3),
            pl.BlockSpec(w_xo.shape, const2),
        ],
        out_specs=pl.BlockSpec((1, tm, D), tile3),
        out_shape=jax.ShapeDtypeStruct((B, S, D), F32),
        scratch_shapes=[pltpu.VMEM((tm, D), F32), pltpu.VMEM((tm, D), F32), pltpu.VMEM((tm, D), BF16),
                        pltpu.VMEM((XATTN_HEADS, tm, M), F32), pltpu.VMEM((XATTN_HEADS, tm, M), BF16),
                        pltpu.VMEM((tm, D), BF16)],
        compiler_params=pltpu.CompilerParams(
            dimension_semantics=("arbitrary", "arbitrary"), vmem_limit_bytes=VMEM_LIMIT),
        name="mix_xattn",
    )(x, pool, attn, w_out, g, w_xq, qg, kx, vx, w_xo)


def _conv_ffn_kernel(x_ref, g_ref, wu_ref, cw_ref, cb_ref, wd_ref, o_ref,
                     hb_ref, halo_ref, buf_ref, act_ref, *, tm, n_chunks):
    s = pl.program_id(1)

    @pl.when(s == 0)
    def _():
        halo_ref[...] = jnp.zeros_like(halo_ref)

    x = x_ref[0]
    hb_ref[...] = _rms(x, g_ref[...]).astype(BF16)

    def conv(c, slot):
        cols = slice(c * FF_CHUNK, (c + 1) * FF_CHUNK)
        up = jnp.dot(hb_ref[...], wu_ref[:, cols], preferred_element_type=F32)
        buf_ref[slot, 0:CONV_HALO, :] = halo_ref[c]
        buf_ref[slot, CONV_HALO:CONV_HALO + tm, :] = up
        halo_ref[c] = up[tm - CONV_HALO:, :]
        cw = cw_ref[:, cols]
        y = up * cw[CONV_WIDTH - 1:CONV_WIDTH, :] + cb_ref[:, cols]
        for d in range(1, CONV_WIDTH):
            y = y + buf_ref[slot, CONV_HALO - d:CONV_HALO - d + tm, :] * cw[CONV_WIDTH - 1 - d:CONV_WIDTH - d, :]
        return y

    for c in range(n_chunks):
        gate = conv(c, 2 * (c % 2))
        val = conv(c + n_chunks, 2 * (c % 2) + 1)
        act_ref[:, c * FF_CHUNK:(c + 1) * FF_CHUNK] = (gate * (1.0 / (1.0 + jnp.exp2(gate * -LOG2E))) * val).astype(BF16)
    o_ref[0] = x + jnp.dot(act_ref[...], wd_ref[...], preferred_element_type=F32)


def _conv_ffn(x, g, w_up, conv_w, conv_b, w_down, *, tm):
    B, S, D = x.shape
    d_ff = w_down.shape[0]
    n_chunks = d_ff // FF_CHUNK
    const2 = lambda b, s: (0, 0)
    tile3 = lambda b, s: (b, s, 0)
    return pl.pallas_call(
        functools.partial(_conv_ffn_kernel, tm=tm, n_chunks=n_chunks),
        grid=(B, S // tm),
        in_specs=[
            pl.BlockSpec((1, tm, D), tile3),
            pl.BlockSpec((1, D), const2),
            pl.BlockSpec(w_up.shape, const2),
            pl.BlockSpec(conv_w.shape, const2),
            pl.BlockSpec(conv_b.shape, const2),
            pl.BlockSpec(w_down.shape, const2),
        ],
        out_specs=pl.BlockSpec((1, tm, D), tile3),
        out_shape=jax.ShapeDtypeStruct((B, S, D), F32),
        scratch_shapes=[
            pltpu.VMEM((tm, D), BF16),
            pltpu.VMEM((2 * n_chunks, CONV_HALO, FF_CHUNK), F32),
            pltpu.VMEM((4, CONV_HALO + tm, FF_CHUNK), F32),
            pltpu.VMEM((tm, d_ff), BF16),
        ],
        compiler_params=pltpu.CompilerParams(
            dimension_semantics=("arbitrary", "arbitrary"), vmem_limit_bytes=VMEM_LIMIT),
        name="conv_ffn",
    )(x, g, w_up, conv_w, conv_b, w_down)


def _rope_tables(S):
    half = HEAD_DIM // 2
    inv_freq = ROPE_THETA ** (-jnp.arange(half, dtype=F32) / half)
    ang = jnp.arange(S).astype(F32)[:, None] * inv_freq[None, :]
    cos, sin = jnp.cos(ang), jnp.sin(ang)
    return (jnp.concatenate([cos, cos, cos, cos], axis=-1),
            jnp.concatenate([-sin, -sin, sin, sin], axis=-1))


def _slab_gain(g):
    lo, hi = g[:HEAD_DIM // 2], g[HEAD_DIM // 2:]
    return jnp.concatenate([lo, lo, hi, hi]).reshape(1, LANES)


def _slab_columns(w):
    rows, width = w.shape
    w = w.reshape(rows, width // LANES, 2, 2, HEAD_DIM // 2)
    return w.transpose(0, 1, 3, 2, 4).reshape(rows, width)


def kernel(x, mem, norm_mix_g, w_in, pool_w, pool_scale, q_norm_g, k_norm_g, w_out, norm_xattn_g, norm_mem_g, w_xq, w_xkv, xq_norm_g, xk_norm_g, w_xo, norm_ffn_g, w_up, conv_w, conv_b, w_down):
    B, S, D = x.shape
    depth = w_in.shape[0]
    d_ff = w_down.shape[1]
    assert all(S % t == 0 for t in (IN_PROJ_TILE, MIX_TILE, FFN_TILE, MOBA_QBLOCKS * MOBA_BLOCK)) and d_ff % FF_CHUNK == 0
    cos, sin = _rope_tables(S)
    row = lambda a: a.reshape(1, -1)
    pool_width = pool_w.shape[1] * pool_w.shape[2]
    attn_width = (w_in.shape[2] - pool_width) // 3
    q0, k0, v0 = pool_width, pool_width + attn_width, pool_width + 2 * attn_width
    for l in range(depth):
        w_in_l = jnp.concatenate([w_in[l][:, :q0], _slab_columns(w_in[l][:, q0:k0]),
                                  _slab_columns(w_in[l][:, k0:v0]), w_in[l][:, v0:]], axis=1).astype(BF16)
        pool, q, k, v, kmean = _in_proj(
            x, row(norm_mix_g[l]), w_in_l, pool_w[l].astype(BF16), row(pool_scale[l]),
            _slab_gain(q_norm_g[l]), _slab_gain(k_norm_g[l]), cos, sin, tm=IN_PROJ_TILE)
        attn = _moba(q, k, v, kmean.reshape(B, S // MOBA_BLOCK, -1), npairs=MOBA_PAIRS)
        kx, vx = _mem_kv(mem, row(norm_mem_g[l]), w_xkv[l].astype(BF16), row(xk_norm_g[l]))
        x = _mix_xattn(x, pool, attn, w_out[l].astype(BF16), row(norm_xattn_g[l]), w_xq[l].astype(BF16),
                       row(xq_norm_g[l]), kx, vx, w_xo[l].astype(BF16), tm=MIX_TILE)
        x = _conv_ffn(x, row(norm_ffn_g[l]), w_up[l].astype(BF16), conv_w[l], row(conv_b[l]), w_down[l].astype(BF16),
                      tm=FFN_TILE)
    return x
```
